```python
import jax
import jax.numpy as jnp
from jax import lax
import numpy as np

D_MODEL = 1024
BATCH = 16
SEQ = 2048
DEPTH = 1
DEC_BATCH = 8
DEC_SEQ = 4096
PAST_LEN = 128

RET_HEADS = 8
RET_DK = 64
RET_DV = 64
RET_WIDTH = RET_HEADS * RET_DV
RET_CHUNK = 128
ROPE_BASE = 10000.0
HG_HEADS = 4
HG_DK = 128
HG_DV = 128
HG_KDIM = HG_HEADS * HG_DK
HG_WIDTH = HG_HEADS * HG_DV
HG_CHUNK = 64
HG_SUB = 8
D_FF = 2816
CONV_W = 3
EPS = 1e-6
IN_SPLITS = (RET_HEADS * RET_DK, RET_HEADS * RET_DK, RET_WIDTH, RET_WIDTH,
             HG_KDIM, HG_KDIM, HG_KDIM, HG_WIDTH, HG_WIDTH, D_MODEL, D_MODEL)
D_IN = 2 * RET_HEADS * RET_DK + 2 * RET_WIDTH + 3 * HG_KDIM + 2 * HG_WIDTH + 2 * D_MODEL
F32 = jnp.float32

kernel_name = "hybrid_retention_hgrn2_convffn_encoder"


def rmsnorm(x, w):
    xf = x.astype(F32)
    y = xf * lax.rsqrt(jnp.mean(xf * xf, axis=-1, keepdims=True) + EPS)
    return (y * w.astype(F32)).astype(x.dtype)


def head_rms(o):
    return o * lax.rsqrt(jnp.mean(o * o, axis=-1, keepdims=True) + EPS)


def split_heads(t, n_heads):
    b, l, _ = t.shape
    return t.reshape(b, l, n_heads, -1).transpose(0, 2, 1, 3).astype(F32)


def merge_heads(t):
    b, h, l, d = t.shape
    return t.transpose(0, 2, 1, 3).reshape(b, l, h * d)


def to_chunks(t, c):
    b, h, l, d = t.shape
    return t.reshape(b, h, l // c, c, d)


def flip_seq(t):
    return jnp.flip(t, axis=2)


def rotary(x):
    l, d = x.shape[-2], x.shape[-1]
    half = d // 2
    inv_freq = ROPE_BASE ** (-jnp.arange(half, dtype=F32) / half)
    ang = jnp.arange(l, dtype=F32)[:, None] * inv_freq[None, :]
    cos, sin = jnp.cos(ang), jnp.sin(ang)
    x1, x2 = x[..., :half], x[..., half:]
    return jnp.concatenate([x1 * cos - x2 * sin, x1 * sin + x2 * cos], axis=-1)


def chunk_state_scan(decay, ds):
    def step(s, inp):
        a, d = inp
        return a * s + d, s
    _, prev = lax.scan(step, jnp.zeros(ds.shape[1:], ds.dtype), (decay, ds))
    return prev


def retention_dir(q, k, v, log_gamma, inclusive):
    C = RET_CHUNK
    qc, kc, vc = to_chunks(q, C), to_chunks(k, C), to_chunks(v, C)
    idx = jnp.arange(C, dtype=F32)
    lg = log_gamma[:, None, None]
    diff = idx[:, None] - idx[None, :]
    mask = (diff >= 0) if inclusive else (diff > 0)
    dmat = jnp.where(mask, jnp.exp(jnp.where(mask, diff, 0.0) * lg), 0.0)
    scores = jnp.einsum('bhnid,bhnjd->bhnij', qc, kc) * dmat[None, :, None]
    o_intra = jnp.einsum('bhnij,bhnje->bhnie', scores, vc)
    k_w = kc * jnp.exp((C - 1.0 - idx)[:, None] * lg)[None, :, None]
    ds = jnp.einsum('bhncd,bhnce->bhnde', k_w, vc)
    ds_n = jnp.moveaxis(ds, 2, 0)
    decay = jnp.broadcast_to(jnp.exp(C * log_gamma)[None, None, :, None, None],
                             (ds_n.shape[0], 1, ds_n.shape[2], 1, 1))
    prev = jnp.moveaxis(chunk_state_scan(decay, ds_n), 0, 2)
    q_w = qc * jnp.exp((idx + 1.0)[:, None] * lg)[None, :, None]
    o_cross = jnp.einsum('bhncd,bhnde->bhnce', q_w, prev)
    return (o_intra + o_cross).reshape(v.shape)


def hgrn2_dir(q, k, v, g):
    C, c = HG_CHUNK, HG_SUB
    ns = C // c
    qc, kc, vc, gc = [to_chunks(t, C) for t in (q, k, v, g)]
    b = jnp.cumsum(gc, axis=-2)
    b_last = b[..., -1:, :]
    ds = jnp.einsum('bhncd,bhnce->bhnde', kc * jnp.exp(b_last - b), vc)
    decay = jnp.swapaxes(jnp.moveaxis(jnp.exp(b_last), 2, 0), -1, -2)
    prev = jnp.moveaxis(chunk_state_scan(decay, jnp.moveaxis(ds, 2, 0)), 0, 2)
    o_cross = jnp.einsum('bhncd,bhnde->bhnce', qc * jnp.exp(b), prev)
    sh = qc.shape[:3] + (ns, c, qc.shape[-1])
    qs, ks, bs = qc.reshape(sh), kc.reshape(sh), b.reshape(sh)
    vs = vc.reshape(qc.shape[:3] + (ns, c, vc.shape[-1]))
    b_start = jnp.concatenate([jnp.zeros_like(bs[..., :1, -1:, :]), bs[..., :-1, -1:, :]],
                              axis=-3)
    q_t = qs * jnp.exp(bs - b_start)
    before = jnp.arange(C)[None, :] < (jnp.arange(ns) * c)[:, None]
    rel = b_start - b[..., None, :, :]
    k_t = kc[..., None, :, :] * jnp.where(before[:, :, None], jnp.exp(jnp.minimum(rel, 0.0)), 0.0)
    a_inter = jnp.einsum('bhnaid,bhnajd->bhnaij', q_t, k_t)
    o_inter = jnp.einsum('bhnaij,bhnje->bhnaie', a_inter, vc)
    tri = jnp.arange(c)[:, None] >= jnp.arange(c)[None, :]
    rel_d = bs[..., :, None, :] - bs[..., None, :, :]
    dec = jnp.where(tri[..., None], jnp.exp(jnp.minimum(rel_d, 0.0)), 0.0)
    a_diag = jnp.einsum('bhnaid,bhnajd,bhnaijd->bhnaij', qs, ks, dec)
    o_diag = jnp.einsum('bhnaij,bhnaje->bhnaie', a_diag, vs)
    o_sub = (o_inter + o_diag).reshape(vc.shape)
    return (o_cross + o_sub).reshape(v.shape)


def token_mixer(hn, w_in, lb, w_ret_norm, w_hg_norm, w_ret_up, w_hg_up, w_out):
    dt = hn.dtype
    proj = hn @ w_in
    offsets = np.cumsum(IN_SPLITS)[:-1].tolist()
    rq, rk, rv, rg, hq, hf_fwd, hf_bwd, hi, hg, ga, gb = jnp.split(proj, offsets, axis=-1)

    q = rotary(split_heads(rq, RET_HEADS))
    k = rotary(split_heads(rk, RET_HEADS)) * RET_DK ** -0.5
    v = split_heads(rv, RET_HEADS)
    log_gamma = jnp.log(1.0 - 2.0 ** (-5.0 - jnp.arange(RET_HEADS, dtype=F32)))
    o_ret = (retention_dir(q, k, v, log_gamma, True)
             + flip_seq(retention_dir(flip_seq(q), flip_seq(k), flip_seq(v), log_gamma, False)))
    ret = merge_heads(head_rms(o_ret)) * w_ret_norm.astype(F32) * jax.nn.silu(rg.astype(F32))

    q2 = jax.nn.silu(split_heads(hq, HG_HEADS))
    i2 = split_heads(hi, HG_HEADS)

    def forget(pre, lower):
        f = lower + (1.0 - lower) * jax.nn.sigmoid(pre.astype(F32))
        return split_heads(1.0 - f, HG_HEADS), split_heads(jnp.log(f), HG_HEADS)

    k_f, g_f = forget(hf_fwd, lb[0])
    k_b, g_b = forget(hf_bwd, lb[1])
    o_hg = (hgrn2_dir(q2, k_f, i2, g_f)
            + flip_seq(hgrn2_dir(flip_seq(q2), flip_seq(k_b), flip_seq(i2), flip_seq(g_b))))
    hgo = merge_heads(head_rms(o_hg)) * w_hg_norm.astype(F32) * jax.nn.silu(hg.astype(F32))

    merged = (jax.nn.sigmoid(ga.astype(F32)) * (ret.astype(dt) @ w_ret_up).astype(F32)
              + jax.nn.sigmoid(gb.astype(F32)) * (hgo.astype(dt) @ w_hg_up).astype(F32))
    return merged.astype(dt) @ w_out


def conv_ffn(hn, w_ffn_in, conv_w, conv_b, w_ffn_down):
    dt = hn.dtype
    u, v = jnp.split(hn @ w_ffn_in, 2, axis=-1)
    pad = CONV_W // 2
    up = jnp.pad(u, ((0, 0), (pad, pad), (0, 0)))
    L = u.shape[1]
    uc = conv_b
    for t in range(CONV_W):
        uc = uc + up[:, t:t + L] * conv_w[t]
    return (jax.nn.gelu(uc) * v).astype(dt) @ w_ffn_down


def encode(x, w_in, lower_bounds, w_ret_norm, w_hg_norm, w_ret_up, w_hg_up, w_out,
           norm_mix, norm_ffn, w_ffn_in, conv_w, conv_b, w_ffn_down, norm_final):
    lbs = jnp.cumsum(jax.nn.softmax(lower_bounds.astype(F32), axis=0), axis=0)
    h = x
    for l in range(DEPTH):
        h = h + token_mixer(rmsnorm(h, norm_mix[l]), w_in[l], lbs[l], w_ret_norm[l],
                            w_hg_norm[l], w_ret_up[l], w_hg_up[l], w_out[l])
        h = h + conv_ffn(rmsnorm(h, norm_ffn[l]), w_ffn_in[l], conv_w[l], conv_b[l],
                         w_ffn_down[l])
    return rmsnorm(h, norm_final)


def setup_inputs(seed: int = 0) -> dict:
    key = jax.random.key(seed)
    ks = jax.random.split(key, 16)

    def dense(k, shape, fan_in):
        return jax.random.normal(k, shape, F32) * fan_in ** -0.5

    def gain(k, shape):
        return 1.0 + 0.02 * jax.random.normal(k, shape, F32)

    return {
        "x_prompt": jax.random.normal(ks[0], (BATCH, SEQ, D_MODEL), F32),
        "x_sample": jax.random.normal(ks[1], (DEC_BATCH, DEC_SEQ, D_MODEL), F32),
        "w_in": dense(ks[2], (DEPTH, D_MODEL, D_IN), D_MODEL),
        "lower_bounds": 0.1 * jax.random.normal(ks[3], (DEPTH + 1, 2, HG_KDIM), F32),
        "w_ret_norm": gain(ks[4], (DEPTH, RET_WIDTH)),
        "w_hg_norm": gain(ks[5], (DEPTH, HG_WIDTH)),
        "w_ret_up": dense(ks[6], (DEPTH, RET_WIDTH, D_MODEL), RET_WIDTH),
        "w_hg_up": dense(ks[7], (DEPTH, HG_WIDTH, D_MODEL), HG_WIDTH),
        "w_out": dense(ks[8], (DEPTH, D_MODEL, D_MODEL), D_MODEL),
        "norm_mix": gain(ks[9], (DEPTH, D_MODEL)),
        "norm_ffn": gain(ks[10], (DEPTH, D_MODEL)),
        "w_ffn_in": dense(ks[11], (DEPTH, D_MODEL, 2 * D_FF), D_MODEL),
        "conv_w": dense(ks[12], (DEPTH, CONV_W, D_FF), CONV_W),
        "conv_b": 0.01 * jax.random.normal(ks[13], (DEPTH, D_FF), F32),
        "w_ffn_down": dense(ks[14], (DEPTH, D_FF, D_MODEL), D_FF),
        "norm_final": gain(ks[15], (D_MODEL,)),
    }


def reference(x_prompt, x_sample, w_in, lower_bounds, w_ret_norm, w_hg_norm, w_ret_up,
              w_hg_up, w_out, norm_mix, norm_ffn, w_ffn_in, conv_w, conv_b, w_ffn_down,
              norm_final):
    y_prompt = encode(x_prompt, w_in, lower_bounds, w_ret_norm, w_hg_norm, w_ret_up, w_hg_up,
                      w_out, norm_mix, norm_ffn, w_ffn_in, conv_w, conv_b, w_ffn_down,
                      norm_final)
    y_sample = encode(x_sample, w_in, lower_bounds, w_ret_norm, w_hg_norm, w_ret_up, w_hg_up,
                      w_out, norm_mix, norm_ffn, w_ffn_in, conv_w, conv_b, w_ffn_down,
                      norm_final)
    return (y_prompt, y_sample)
```

```python
import functools

import jax
import jax.numpy as jnp
from jax import lax
from jax.experimental import pallas as pl
from jax.experimental.pallas import tpu as pltpu

F32 = jnp.float32
BF16 = jnp.bfloat16

D_MODEL = 1024
RET_HEADS = 8
RET_DK = 64
RET_WIDTH = 512
HG_HEADS = 4
HG_DK = 128
HG_WIDTH = 512
D_FF = 2816
EPS = 1e-6
ROPE_BASE = 10000.0

C_RQ, C_RK, C_RV, C_RG = 0, 512, 1024, 1536
C_HQ, C_HFF, C_HFB, C_HI, C_HG = 2048, 2560, 3072, 3584, 4096
C_GA, C_GB, D_IN = 4608, 5632, 6656

LANES = 128
SUB = 8
CHUNK = 128
TILE = 256
FFN_TILE = 256
VMEM_LIMIT = 56 * 1024 * 1024


def _dot(a, b):
    return jnp.dot(a, b, preferred_element_type=F32)


def _dot_nt(a, b):
    return lax.dot_general(a, b, (((1,), (1,)), ((), ())), preferred_element_type=F32)


def _dot_tn(a, b):
    return lax.dot_general(a, b, (((0,), (0,)), ((), ())), preferred_element_type=F32)


def _sigmoid(x):
    return 1.0 / (1.0 + jnp.exp(-x))


def _silu(x):
    return x * _sigmoid(x)


def _rmsnorm(x, w):
    return x * lax.rsqrt(jnp.mean(x * x, axis=-1, keepdims=True) + EPS) * w


def _iota(shape, dim):
    return lax.broadcasted_iota(jnp.int32, shape, dim)


def _lower_bound(lb_ref, direction):
    a0 = lb_ref[0, direction:direction + 1, :]
    a1 = lb_ref[1, direction:direction + 1, :]
    m = jnp.maximum(a0, a1)
    e0 = jnp.exp(a0 - m)
    e1 = jnp.exp(a1 - m)
    return e0 / (e0 + e1)


def _forget(pre, lower):
    f = lower + (1.0 - lower) * _sigmoid(pre)
    return 1.0 - f, jnp.log(f)


def _rotary(x, cos, sin_signed):
    first_half = (_iota(x.shape, 1) % RET_DK) < (RET_DK // 2)
    swapped = jnp.where(first_half, pltpu.roll(x, LANES - RET_DK // 2, axis=1),
                        pltpu.roll(x, RET_DK // 2, axis=1))
    return x * cos + swapped * sin_signed


def _hgrn_chunk(q, k, g, v, st_ref, head, rev):
    c = q.shape[0]
    row = _iota((c, c), 0)
    col = _iota((c, c), 1)
    tri = jnp.where((col >= row) if rev else (col <= row), 1.0, 0.0).astype(BF16)
    g_hi = g.astype(BF16)
    g_lo = (g - g_hi.astype(F32)).astype(BF16)
    b = _dot(tri, g_hi) + _dot(tri, g_lo)

    sub = _iota((c, 1), 0) % SUB
    o = jnp.sum(q * k, axis=-1, keepdims=True) * v
    for d in range(1, SUB):
        shift = (c - d) if rev else d
        ks = pltpu.roll(k, shift, axis=0)
        bs = pltpu.roll(b, shift, axis=0)
        vs = pltpu.roll(v, shift, axis=0)
        a = jnp.sum(q * ks * jnp.exp(jnp.minimum(b - bs, 0.0)), axis=-1, keepdims=True)
        valid = (sub <= SUB - 1 - d) if rev else (sub >= d)
        o = o + jnp.where(valid, a, 0.0) * vs

    amat = jnp.zeros((c, c), F32)
    s = SUB
    while s < c:
        nblk = c // s
        if rev:
            edge = [jnp.broadcast_to(b[m * s:m * s + 1, :], (s, LANES)) for m in range(nblk)]
            k_edge = jnp.concatenate(edge, axis=0)
            q_edge = jnp.concatenate(edge[1:] + [jnp.zeros((s, LANES), F32)], axis=0)
        else:
            edge = [jnp.broadcast_to(b[m * s + s - 1:m * s + s, :], (s, LANES)) for m in range(nblk)]
            k_edge = jnp.concatenate(edge, axis=0)
            q_edge = jnp.concatenate([jnp.zeros((s, LANES), F32)] + edge[:-1], axis=0)
        qt = (q * jnp.exp(b - q_edge)).astype(BF16)
        kt = (k * jnp.exp(k_edge - b)).astype(BF16)
        shift_bits = s.bit_length() - 1
        rb = row >> shift_bits
        cb = col >> shift_bits
        if rev:
            pair = ((rb & 1) == 0) & (cb == rb + 1)
        else:
            pair = ((rb & 1) == 1) & (cb == rb - 1)
        amat = jnp.where(pair, _dot_nt(qt, kt), amat)
        s *= 2
    vb = v.astype(BF16)
    o = o + _dot(amat.astype(BF16), vb)

    b_tot = b[0:1, :] if rev else b[c - 1:c, :]
    st = st_ref[head]
    o = o + _dot_nt((q * jnp.exp(b)).astype(BF16), st.astype(BF16))
    ds_t = _dot_tn(vb, (k * jnp.exp(b_tot - b)).astype(BF16))
    st_ref[head] = st * jnp.exp(b_tot) + ds_t
    return o


def _ret_state_step(q, k, vb, wq, wk, gblk, st_ref, pair):
    st = st_ref[pair]
    o = _dot((q * wq).astype(BF16), st.astype(BF16))
    ds = _dot_tn((k * wk).astype(BF16), vb)
    same_head = (_iota((LANES, LANES), 0) < RET_DK) == (_iota((LANES, LANES), 1) < RET_DK)
    st_ref[pair] = st * gblk + jnp.where(same_head, ds, 0.0)
    return o


def _mix_fwd_kernel(x_ref, nmix_ref, w_ref, lb_ref, cos_ref, sin_ref, dmat_ref, wq_ref, wk_ref,
                    gblk_ref, pret_ref, phg_ref, hfb_ref, gates_ref, of_ref, sret_ref, shg_ref):
    @pl.when(pl.program_id(1) == 0)
    def _():
        sret_ref[...] = jnp.zeros_like(sret_ref)
        shg_ref[...] = jnp.zeros_like(shg_ref)

    tm = x_ref.shape[1]
    hn = _rmsnorm(x_ref[0], nmix_ref[...]).astype(BF16)

    def proj(lo, hi):
        return _dot(hn, w_ref[:, lo:hi])

    gates_ref[0, :, 0:RET_WIDTH] = proj(C_RG, C_RG + RET_WIDTH).astype(BF16)
    gates_ref[0, :, RET_WIDTH:] = proj(C_HG, D_IN).astype(BF16)
    hfb_ref[0] = proj(C_HFB, C_HFB + HG_WIDTH)

    pq = proj(C_RQ, C_RQ + RET_WIDTH)
    pk = proj(C_RK, C_RK + RET_WIDTH)
    pv = proj(C_RV, C_RV + RET_WIDTH)
    cos = cos_ref[...]
    sin = sin_ref[...]
    head0 = _iota((CHUNK, LANES), 1) < RET_DK
    for p in range(RET_HEADS // 2):
        cols = slice(p * LANES, (p + 1) * LANES)
        q = _rotary(pq[:, cols], cos, sin)
        k = _rotary(pk[:, cols], cos, sin) * (RET_DK ** -0.5)
        v = pv[:, cols]
        pret_ref[0, :, p * LANES:(p + 1) * LANES] = q.astype(BF16)
        pret_ref[0, :, RET_WIDTH + p * LANES:RET_WIDTH + (p + 1) * LANES] = k.astype(BF16)
        pret_ref[0, :, 2 * RET_WIDTH + p * LANES:2 * RET_WIDTH + (p + 1) * LANES] = v.astype(BF16)
        for ci in range(tm // CHUNK):
            rows = slice(ci * CHUNK, (ci + 1) * CHUNK)
            qc, kc, vc = q[rows], k[rows], v[rows]
            kb = kc.astype(BF16)
            vb = vc.astype(BF16)
            s0 = _dot_nt(jnp.where(head0, qc, 0.0).astype(BF16), kb) * dmat_ref[2 * p]
            s1 = _dot_nt(jnp.where(head0, 0.0, qc).astype(BF16), kb) * dmat_ref[2 * p + 1]
            pcat = jnp.concatenate([s0.astype(BF16), s1.astype(BF16)], axis=1)
            zero = jnp.zeros_like(vb)
            vblk = jnp.concatenate([jnp.where(head0, vb, zero), jnp.where(head0, zero, vb)], axis=0)
            o = _dot(pcat, vblk)
            o = o + _ret_state_step(qc, kc, vb, wq_ref[:, cols], wk_ref[:, cols], gblk_ref[p],
                                    sret_ref, p)
            of_ref[0, ci * CHUNK:(ci + 1) * CHUNK, p * LANES:(p + 1) * LANES] = o

    q2 = _silu(proj(C_HQ, C_HQ + HG_WIDTH))
    vi = proj(C_HI, C_HI + HG_WIDTH)
    phg_ref[0, :, 0:HG_WIDTH] = q2.astype(BF16)
    phg_ref[0, :, HG_WIDTH:] = vi.astype(BF16)
    kf, gf = _forget(proj(C_HFF, C_HFF + HG_WIDTH), _lower_bound(lb_ref, 0))
    for h in range(HG_HEADS):
        cols = slice(h * LANES, (h + 1) * LANES)
        for ci in range(tm // CHUNK):
            rows = slice(ci * CHUNK, (ci + 1) * CHUNK)
            o = _hgrn_chunk(q2[rows, cols], kf[rows, cols], gf[rows, cols], vi[rows, cols],
                            shg_ref, h, rev=False)
            of_ref[0, ci * CHUNK:(ci + 1) * CHUNK,
                   RET_WIDTH + h * LANES:RET_WIDTH + (h + 1) * LANES] = o


def _mix_bwd_kernel(x_ref, pret_ref, phg_ref, hfb_ref, gates_ref, of_ref, lb_ref, wq_ref, wk_ref,
                    gblk_ref, rnorm_ref, hnorm_ref, wru_ref, whu_ref, wout_ref,
                    h1_ref, sret_ref, shg_ref, act_ref):
    @pl.when(pl.program_id(1) == 0)
    def _():
        sret_ref[...] = jnp.zeros_like(sret_ref)
        shg_ref[...] = jnp.zeros_like(shg_ref)

    tm = x_ref.shape[1]
    nchunk = tm // CHUNK
    head0 = _iota((CHUNK, LANES), 1) < RET_DK

    for p in range(RET_HEADS // 2):
        cols = slice(p * LANES, (p + 1) * LANES)
        for ci in reversed(range(nchunk)):
            rows = slice(ci * CHUNK, (ci + 1) * CHUNK)
            qc = pret_ref[0, rows, p * LANES:(p + 1) * LANES].astype(F32)
            kc = pret_ref[0, rows, RET_WIDTH + p * LANES:RET_WIDTH + (p + 1) * LANES].astype(F32)
            vb = pret_ref[0, rows, 2 * RET_WIDTH + p * LANES:2 * RET_WIDTH + (p + 1) * LANES]
            o = of_ref[0, rows, cols] + _ret_state_step(qc, kc, vb, wq_ref[:, cols], wk_ref[:, cols],
                                                        gblk_ref[p], sret_ref, p)
            sq = o * o
            ms0 = jnp.sum(jnp.where(head0, sq, 0.0), axis=-1, keepdims=True) * (1.0 / RET_DK)
            ms1 = jnp.sum(jnp.where(head0, 0.0, sq), axis=-1, keepdims=True) * (1.0 / RET_DK)
            o = o * jnp.where(head0, lax.rsqrt(ms0 + EPS), lax.rsqrt(ms1 + EPS))
            gate = _silu(gates_ref[0, rows, cols].astype(F32))
            act_ref[rows, cols] = (o * rnorm_ref[:, cols] * gate).astype(BF16)

    lower = _lower_bound(lb_ref, 1)
    for h in range(HG_HEADS):
        cols = slice(h * LANES, (h + 1) * LANES)
        hcols = slice(RET_WIDTH + h * LANES, RET_WIDTH + (h + 1) * LANES)
        for ci in reversed(range(nchunk)):
            rows = slice(ci * CHUNK, (ci + 1) * CHUNK)
            kb, gb = _forget(hfb_ref[0, rows, cols], lower[:, cols])
            q2 = phg_ref[0, rows, cols].astype(F32)
            vi = phg_ref[0, rows, HG_WIDTH + h * LANES:HG_WIDTH + (h + 1) * LANES].astype(F32)
            o = of_ref[0, rows, hcols] + _hgrn_chunk(q2, kb, gb, vi, shg_ref, h, rev=True)
            o = o * lax.rsqrt(jnp.mean(o * o, axis=-1, keepdims=True) + EPS)
            gate = _silu(gates_ref[0, rows, hcols].astype(F32))
            act_ref[rows, hcols] = (o * hnorm_ref[:, cols] * gate).astype(BF16)

    up_ret = _dot(act_ref[:, 0:RET_WIDTH], wru_ref[...])
    up_hg = _dot(act_ref[:, RET_WIDTH:], whu_ref[...])
    ga = gates_ref[0, :, 2 * RET_WIDTH:2 * RET_WIDTH + D_MODEL].astype(F32)
    gb_ = gates_ref[0, :, 2 * RET_WIDTH + D_MODEL:].astype(F32)
    merged = (_sigmoid(ga) * up_ret + _sigmoid(gb_) * up_hg).astype(BF16)
    h1_ref[0] = x_ref[0] + _dot(merged, wout_ref[...])


def _gelu_tanh(x):
    return 0.5 * x * (1.0 + jnp.tanh(0.7978845608028654 * (x + 0.044715 * (x * x * x))))


def _ffn_kernel(hin_ref, hres_ref, nffn_ref, win_ref, cw_ref, cb_ref, wdown_ref, nfin_ref,
                y_ref, u_ref, v_ref):
    j = pl.program_id(1)
    nt = pl.num_programs(1) - 1
    tm = hin_ref.shape[1]
    slot = j % 2
    prev = 1 - slot

    @pl.when(j < nt)
    def _():
        hn = _rmsnorm(hin_ref[0], nffn_ref[...]).astype(BF16)
        u_ref[slot, SUB:SUB + tm, :] = _dot(hn, win_ref[:, 0:D_FF])
        v_ref[slot] = _dot(hn, win_ref[:, D_FF:])

    @pl.when(j == nt)
    def _():
        u_ref[slot, SUB:2 * SUB, :] = jnp.zeros((SUB, D_FF), F32)

    @pl.when(j == 1)
    def _():
        u_ref[prev, 0:SUB, :] = jnp.zeros((SUB, D_FF), F32)

    @pl.when(j >= 1)
    def _():
        u_ref[prev, SUB + tm:2 * SUB + tm, :] = u_ref[slot, SUB:2 * SUB, :]
        uc = (cb_ref[...] + u_ref[prev, SUB - 1:SUB - 1 + tm, :] * cw_ref[0:1, :]
              + u_ref[prev, SUB:SUB + tm, :] * cw_ref[1:2, :]
              + u_ref[prev, SUB + 1:SUB + 1 + tm, :] * cw_ref[2:3, :])
        act = (_gelu_tanh(uc) * v_ref[prev]).astype(BF16)
        h2 = hres_ref[0] + _dot(act, wdown_ref[...])
        y_ref[0] = _rmsnorm(h2, nfin_ref[...])
        u_ref[slot, 0:SUB, :] = u_ref[prev, tm:SUB + tm, :]


def _const_spec(shape):
    zeros = (0,) * len(shape)
    return pl.BlockSpec(shape, lambda b, j: zeros, pipeline_mode=pl.Buffered(1))


def _retention_tables(seq_len):
    half = RET_DK // 2
    inv_freq = ROPE_BASE ** (-jnp.arange(half, dtype=F32) / half)
    ang = jnp.arange(seq_len, dtype=F32)[:, None] * inv_freq[None, :]
    cos, sin = jnp.cos(ang), jnp.sin(ang)
    cos_t = jnp.tile(cos, (1, LANES // half))
    sin_t = jnp.tile(jnp.concatenate([-sin, sin], axis=1), (1, LANES // RET_DK))
    log_gamma = jnp.log(1.0 - 2.0 ** (-5.0 - jnp.arange(RET_HEADS, dtype=F32)))
    idx = jnp.arange(CHUNK, dtype=F32)
    dist = jnp.abs(idx[:, None] - idx[None, :])
    dmat = jnp.exp(dist[None] * log_gamma[:, None, None])
    lg_cols = jnp.repeat(log_gamma, RET_DK)[None, :]
    wq_f = jnp.exp((idx + 1.0)[:, None] * lg_cols)
    wk_f = jnp.exp((CHUNK - 1.0 - idx)[:, None] * lg_cols)
    wq_b = jnp.exp((CHUNK - idx)[:, None] * lg_cols)
    wk_b = jnp.exp(idx[:, None] * lg_cols)
    lane_head = jnp.arange(LANES) // RET_DK
    gdec = jnp.exp(CHUNK * log_gamma).reshape(RET_HEADS // 2, 2)
    same = lane_head[:, None] == lane_head[None, :]
    gblk = jnp.where(same[None], gdec[:, lane_head][:, :, None], 0.0)
    return cos_t, sin_t, dmat, wq_f, wk_f, wq_b, wk_b, gblk


def _encode(x, p):
    bsz, seq_len, _ = x.shape
    assert seq_len % TILE == 0 and seq_len % FFN_TILE == 0
    nt = seq_len // TILE
    cos_t, sin_t, dmat, wq_f, wk_f, wq_b, wk_b, gblk = _retention_tables(seq_len)
    params = pltpu.CompilerParams(dimension_semantics=("parallel", "arbitrary"),
                                  vmem_limit_bytes=VMEM_LIMIT)

    def tile_spec(width, rev=False):
        if rev:
            return pl.BlockSpec((1, TILE, width), lambda b, j: (b, nt - 1 - j, 0))
        return pl.BlockSpec((1, TILE, width), lambda b, j: (b, j, 0))

    def out(width, dtype):
        return jax.ShapeDtypeStruct((bsz, seq_len, width), dtype)

    state_scratch = [pltpu.VMEM((RET_HEADS // 2, LANES, LANES), F32),
                     pltpu.VMEM((HG_HEADS, LANES, LANES), F32)]

    pret, phg, hfb, gates, o_fwd = pl.pallas_call(
        _mix_fwd_kernel,
        grid=(bsz, nt),
        in_specs=[tile_spec(D_MODEL), _const_spec((1, D_MODEL)), _const_spec((D_MODEL, D_IN)),
                  _const_spec((2, 2, HG_WIDTH)),
                  pl.BlockSpec((TILE, LANES), lambda b, j: (j, 0)),
                  pl.BlockSpec((TILE, LANES), lambda b, j: (j, 0)),
                  _const_spec((RET_HEADS, CHUNK, CHUNK)), _const_spec((CHUNK, RET_WIDTH)),
                  _const_spec((CHUNK, RET_WIDTH)), _const_spec((RET_HEADS // 2, LANES, LANES))],
        out_specs=[tile_spec(3 * RET_WIDTH), tile_spec(2 * HG_WIDTH), tile_spec(HG_WIDTH),
                   tile_spec(2 * RET_WIDTH + 2 * D_MODEL), tile_spec(RET_WIDTH + HG_WIDTH)],
        out_shape=[out(3 * RET_WIDTH, BF16), out(2 * HG_WIDTH, BF16), out(HG_WIDTH, F32),
                   out(2 * RET_WIDTH + 2 * D_MODEL, BF16), out(RET_WIDTH + HG_WIDTH, F32)],
        scratch_shapes=state_scratch,
        compiler_params=params,
        name="mix_fwd",
    )(x, p["norm_mix"], p["w_in"], p["lower_bounds"], cos_t, sin_t, dmat, wq_f, wk_f, gblk)

    h1 = pl.pallas_call(
        _mix_bwd_kernel,
        grid=(bsz, nt),
        in_specs=[tile_spec(D_MODEL, True), tile_spec(3 * RET_WIDTH, True),
                  tile_spec(2 * HG_WIDTH, True), tile_spec(HG_WIDTH, True),
                  tile_spec(2 * RET_WIDTH + 2 * D_MODEL, True), tile_spec(RET_WIDTH + HG_WIDTH, True),
                  _const_spec((2, 2, HG_WIDTH)), _const_spec((CHUNK, RET_WIDTH)),
                  _const_spec((CHUNK, RET_WIDTH)), _const_spec((RET_HEADS // 2, LANES, LANES)),
                  _const_spec((1, RET_WIDTH)), _const_spec((1, HG_WIDTH)),
                  _const_spec((RET_WIDTH, D_MODEL)), _const_spec((HG_WIDTH, D_MODEL)),
                  _const_spec((D_MODEL, D_MODEL))],
        out_specs=tile_spec(D_MODEL, True),
        out_shape=out(D_MODEL, F32),
        scratch_shapes=state_scratch + [pltpu.VMEM((TILE, RET_WIDTH + HG_WIDTH), BF16)],
        compiler_params=params,
        name="mix_bwd",
    )(x, pret, phg, hfb, gates, o_fwd, p["lower_bounds"], wq_b, wk_b, gblk,
      p["w_ret_norm"], p["w_hg_norm"], p["w_ret_up"], p["w_hg_up"], p["w_out"])

    nf = seq_len // FFN_TILE
    return pl.pallas_call(
        _ffn_kernel,
        grid=(bsz, nf + 1),
        in_specs=[pl.BlockSpec((1, FFN_TILE, D_MODEL), lambda b, j: (b, jnp.minimum(j, nf - 1), 0)),
                  pl.BlockSpec((1, FFN_TILE, D_MODEL), lambda b, j: (b, jnp.maximum(j - 1, 0), 0)),
                  _const_spec((1, D_MODEL)), _const_spec((D_MODEL, 2 * D_FF)),
                  _const_spec((3, D_FF)), _const_spec((1, D_FF)), _const_spec((D_FF, D_MODEL)),
                  _const_spec((1, D_MODEL))],
        out_specs=pl.BlockSpec((1, FFN_TILE, D_MODEL), lambda b, j: (b, jnp.maximum(j - 1, 0), 0)),
        out_shape=out(D_MODEL, F32),
        scratch_shapes=[pltpu.VMEM((2, FFN_TILE + 2 * SUB, D_FF), F32),
                        pltpu.VMEM((2, FFN_TILE, D_FF), F32)],
        compiler_params=params,
        name="conv_ffn",
    )(h1, h1, p["norm_ffn"], p["w_ffn_in"], p["conv_w"], p["conv_b"], p["w_ffn_down"],
      p["norm_final"])


def kernel(x_prompt, x_sample, w_in, lower_bounds, w_ret_norm, w_hg_norm, w_ret_up, w_hg_up, w_out,
           norm_mix, norm_ffn, w_ffn_in, conv_w, conv_b, w_ffn_down, norm_final):
    assert w_in.shape[0] == 1, "one layer"
    p = {
        "w_in": w_in[0].astype(BF16),
        "lower_bounds": lower_bounds.astype(F32),
        "w_ret_norm": w_ret_norm.astype(F32),
        "w_hg_norm": w_hg_norm.astype(F32),
        "w_ret_up": w_ret_up[0].astype(BF16),
        "w_hg_up": w_hg_up[0].astype(BF16),
        "w_out": w_out[0].astype(BF16),
        "norm_mix": norm_mix.astype(F32),
        "norm_ffn": norm_ffn.astype(F32),
        "w_ffn_in": w_ffn_in[0].astype(BF16),
        "conv_w": conv_w[0].astype(F32),
        "conv_b": conv_b.astype(F32),
        "w_ffn_down": w_ffn_down[0].astype(BF16),
        "norm_final": norm_final.reshape(1, D_MODEL).astype(F32),
    }
    return _encode(x_prompt, p), _encode(x_sample, p)
```

```python
import functools

import jax
import jax.numpy as jnp
from jax import lax
from jax.experimental import pallas as pl
from jax.experimental.pallas import tpu as pltpu

F32 = jnp.float32
BF16 = jnp.bfloat16

D_MODEL = 1024
RET_HEADS = 8
RET_DK = 64
RET_WIDTH = 512
HG_HEADS = 4
HG_DK = 128
HG_WIDTH = 512
D_FF = 2816
EPS = 1e-6
ROPE_BASE = 10000.0

C_RQ, C_RK, C_RV, C_RG = 0, 512, 1024, 1536
C_HQ, C_HFF, C_HFB, C_HI, C_HG = 2048, 2560, 3072, 3584, 4096
C_GA, C_GB, D_IN = 4608, 5632, 6656

LANES = 128
SUB = 8
CHUNK = 128
TILE = 256
FFN_TILE = 256
FFN_COLS = 256
VMEM_LIMIT = 56 * 1024 * 1024


def _dot(a, b):
    return jnp.dot(a, b, preferred_element_type=F32)


def _dot_nt(a, b):
    return lax.dot_general(a, b, (((1,), (1,)), ((), ())), preferred_element_type=F32)


def _dot_tn(a, b):
    return lax.dot_general(a, b, (((0,), (0,)), ((), ())), preferred_element_type=F32)


def _sigmoid(x):
    return 1.0 / (1.0 + jnp.exp(-x))


def _silu(x):
    return x * _sigmoid(x)


def _rmsnorm(x, w):
    return x * lax.rsqrt(jnp.mean(x * x, axis=-1, keepdims=True) + EPS) * w


def _iota(shape, dim):
    return lax.broadcasted_iota(jnp.int32, shape, dim)


def _lower_bound(lb_ref, direction):
    a0 = lb_ref[0, direction:direction + 1, :]
    a1 = lb_ref[1, direction:direction + 1, :]
    m = jnp.maximum(a0, a1)
    e0 = jnp.exp(a0 - m)
    e1 = jnp.exp(a1 - m)
    return e0 / (e0 + e1)


def _forget(pre, lower):
    f = lower + (1.0 - lower) * _sigmoid(pre)
    return 1.0 - f, jnp.log2(f)


def _rotary(x, cos, sin_signed):
    first_half = (_iota(x.shape, 1) % RET_DK) < (RET_DK // 2)
    swapped = jnp.where(first_half, pltpu.roll(x, LANES - RET_DK // 2, axis=1),
                        pltpu.roll(x, RET_DK // 2, axis=1))
    return x * cos + swapped * sin_signed


def _pair_codes(rev):
    r = jnp.arange(CHUNK, dtype=jnp.int32)[:, None]
    c = jnp.arange(CHUNK, dtype=jnp.int32)[None, :]
    if rev:
        r, c = c, r
    code = jnp.where((r // SUB == c // SUB) & (r >= c), r - c, -1)
    s, level = SUB, 0
    while s < CHUNK:
        sibling = (r // (2 * s) == c // (2 * s)) & (r // s == c // s + 1)
        code = jnp.where(sibling, SUB + level, code)
        s, level = 2 * s, level + 1
    return code


def _hgrn_chunk(load, emit, st_ref, head, tri_ref, code_ref, rev):
    q, k, g, v = load()
    c = q.shape[0]
    nsub = c // SUB
    g_hi = g.astype(BF16)
    g_lo = (g - g_hi.astype(F32)).astype(BF16)
    b = _dot(tri_ref[...], g_hi) + _dot(tri_ref[...], g_lo)
    yield

    amat = jnp.where(code_ref[...] == 0, jnp.sum(q * k, axis=-1, keepdims=True), 0.0)
    q3 = q.reshape(nsub, SUB, LANES)
    k3 = k.reshape(nsub, SUB, LANES)
    b3 = b.reshape(nsub, SUB, LANES)
    for d in range(1, SUB):
        shift = (SUB - d) if rev else d
        ks = pltpu.roll(k3, shift, axis=1)
        bs = pltpu.roll(b3, shift, axis=1)
        a = jnp.sum(q3 * ks * jnp.exp2(jnp.minimum(b3 - bs, 0.0)), axis=-1, keepdims=True)
        amat = jnp.where(code_ref[...] == d, a.reshape(c, 1), amat)
    yield

    level_dots = []
    s = SUB
    while s < c:
        nblk = c // s
        if rev:
            edge = [jnp.broadcast_to(b[m * s:m * s + 1, :], (s, LANES)) for m in range(nblk)]
            q_edge = jnp.concatenate(edge[1:] + [jnp.zeros((s, LANES), F32)], axis=0)
        else:
            edge = [jnp.broadcast_to(b[m * s + s - 1:m * s + s, :], (s, LANES)) for m in range(nblk)]
            q_edge = jnp.concatenate([jnp.zeros((s, LANES), F32)] + edge[:-1], axis=0)
        k_edge = jnp.concatenate(edge, axis=0)
        qt = (q * jnp.exp2(b - q_edge)).astype(BF16)
        kt = (k * jnp.exp2(k_edge - b)).astype(BF16)
        level_dots.append(_dot_nt(qt, kt))
        s *= 2

    vb = v.astype(BF16)
    b_tot = b[0:1, :] if rev else b[c - 1:c, :]
    st = st_ref[head]
    o_cross = _dot_nt((q * jnp.exp2(b)).astype(BF16), st.astype(BF16))
    ds_t = _dot_tn(vb, (k * jnp.exp2(b_tot - b)).astype(BF16))
    st_ref[head] = st * jnp.exp2(b_tot) + ds_t
    yield

    for level, a in enumerate(level_dots):
        amat = jnp.where(code_ref[...] == SUB + level, a, amat)
    emit(o_cross + _dot(amat.astype(BF16), vb))


def _interleave(groups, fillers):
    fillers = list(fillers)
    for group in groups:
        active = list(group)
        while active:
            still = []
            for gen in active:
                try:
                    next(gen)
                    still.append(gen)
                except StopIteration:
                    pass
            active = still
            if fillers:
                fillers.pop(0)()
    for filler in fillers:
        filler()


def _ret_state_step(q, k, vb, wq, wk, gblk, st_ref, pair):
    st = st_ref[pair]
    o = _dot((q * wq).astype(BF16), st.astype(BF16))
    ds = _dot_tn((k * wk).astype(BF16), vb)
    same_head = (_iota((LANES, LANES), 0) < RET_DK) == (_iota((LANES, LANES), 1) < RET_DK)
    st_ref[pair] = st * gblk + jnp.where(same_head, ds, 0.0)
    return o


def _mix_fwd_kernel(x_ref, nmix_ref, w_ref, lb_ref, cos_ref, sin_ref, dmat_ref, wq_ref, wk_ref,
                    gblk_ref, tri_ref, code_ref, pret_ref, phg_ref, hfb_ref, gates_ref, of_ref,
                    sret_ref, shg_ref):
    @pl.when(pl.program_id(1) == 0)
    def _():
        sret_ref[...] = jnp.zeros_like(sret_ref)
        shg_ref[...] = jnp.zeros_like(shg_ref)

    tm = x_ref.shape[1]
    nchunk = tm // CHUNK
    hn = _rmsnorm(x_ref[0], nmix_ref[...]).astype(BF16)

    def proj(lo, width):
        return _dot(hn, w_ref[:, lo:lo + width])

    q2 = _silu(proj(C_HQ, HG_WIDTH))
    vi = proj(C_HI, HG_WIDTH)
    phg_ref[0, :, 0:HG_WIDTH] = q2.astype(BF16)
    phg_ref[0, :, HG_WIDTH:] = vi.astype(BF16)
    kf, gf = _forget(proj(C_HFF, HG_WIDTH), _lower_bound(lb_ref, 0))

    def hgrn_item(h, ci):
        cols = slice(h * LANES, (h + 1) * LANES)
        rows = slice(ci * CHUNK, (ci + 1) * CHUNK)

        def load():
            return q2[rows, cols], kf[rows, cols], gf[rows, cols], vi[rows, cols]

        def emit(o):
            of_ref[0, rows, RET_WIDTH + h * LANES:RET_WIDTH + (h + 1) * LANES] = o

        return _hgrn_chunk(load, emit, shg_ref, h, tri_ref, code_ref, rev=False)

    def gate_item(src, dst):
        gates_ref[0, :, dst:dst + RET_WIDTH] = proj(src, RET_WIDTH).astype(BF16)

    def hfb_item():
        hfb_ref[0] = proj(C_HFB, HG_WIDTH)

    def ret_item(group):
        width = 2 * LANES
        pq = proj(C_RQ + group * width, width)
        pk = proj(C_RK + group * width, width)
        pv = proj(C_RV + group * width, width)
        cos = cos_ref[...]
        sin = sin_ref[...]
        head0 = _iota((CHUNK, LANES), 1) < RET_DK
        for pi in range(2):
            p = 2 * group + pi
            cols = slice(p * LANES, (p + 1) * LANES)
            lcols = slice(pi * LANES, (pi + 1) * LANES)
            q = _rotary(pq[:, lcols], cos, sin)
            k = _rotary(pk[:, lcols], cos, sin) * (RET_DK ** -0.5)
            v = pv[:, lcols]
            pret_ref[0, :, p * LANES:(p + 1) * LANES] = q.astype(BF16)
            pret_ref[0, :, RET_WIDTH + p * LANES:RET_WIDTH + (p + 1) * LANES] = k.astype(BF16)
            pret_ref[0, :, 2 * RET_WIDTH + p * LANES:2 * RET_WIDTH + (p + 1) * LANES] = v.astype(BF16)
            for ci in range(nchunk):
                rows = slice(ci * CHUNK, (ci + 1) * CHUNK)
                qc, kc, vc = q[rows], k[rows], v[rows]
                kb = kc.astype(BF16)
                vb = vc.astype(BF16)
                s0 = _dot_nt(jnp.where(head0, qc, 0.0).astype(BF16), kb) * dmat_ref[2 * p]
                s1 = _dot_nt(jnp.where(head0, 0.0, qc).astype(BF16), kb) * dmat_ref[2 * p + 1]
                pcat = jnp.concatenate([s0.astype(BF16), s1.astype(BF16)], axis=1)
                zero = jnp.zeros_like(vb)
                vblk = jnp.concatenate([jnp.where(head0, vb, zero), jnp.where(head0, zero, vb)],
                                       axis=0)
                o = _dot(pcat, vblk)
                o = o + _ret_state_step(qc, kc, vb, wq_ref[:, cols], wk_ref[:, cols], gblk_ref[p],
                                        sret_ref, p)
                of_ref[0, rows, cols] = o

    mxu_items = [functools.partial(ret_item, 0),
                 functools.partial(gate_item, C_RG, 0),
                 functools.partial(gate_item, C_HG, RET_WIDTH),
                 functools.partial(ret_item, 1),
                 functools.partial(gate_item, C_GA, 2 * RET_WIDTH),
                 functools.partial(gate_item, C_GA + RET_WIDTH, 3 * RET_WIDTH),
                 functools.partial(gate_item, C_GB, 4 * RET_WIDTH),
                 functools.partial(gate_item, C_GB + RET_WIDTH, 5 * RET_WIDTH),
                 hfb_item]
    _interleave([[hgrn_item(h, ci) for h in range(HG_HEADS)] for ci in range(nchunk)], mxu_items)


def _mix_bwd_kernel(x_ref, pret_ref, phg_ref, hfb_ref, gates_ref, of_ref, lb_ref, wq_ref, wk_ref,
                    gblk_ref, tri_ref, code_ref, rnorm_ref, hnorm_ref, wru_ref, whu_ref, wout_ref,
                    h1_ref, sret_ref, shg_ref, act_ref):
    @pl.when(pl.program_id(1) == 0)
    def _():
        sret_ref[...] = jnp.zeros_like(sret_ref)
        shg_ref[...] = jnp.zeros_like(shg_ref)

    tm = x_ref.shape[1]
    nchunk = tm // CHUNK
    head0 = _iota((CHUNK, LANES), 1) < RET_DK

    def ret_item(p, ci):
        cols = slice(p * LANES, (p + 1) * LANES)
        rows = slice(ci * CHUNK, (ci + 1) * CHUNK)
        qc = pret_ref[0, rows, p * LANES:(p + 1) * LANES].astype(F32)
        kc = pret_ref[0, rows, RET_WIDTH + p * LANES:RET_WIDTH + (p + 1) * LANES].astype(F32)
        vb = pret_ref[0, rows, 2 * RET_WIDTH + p * LANES:2 * RET_WIDTH + (p + 1) * LANES]
        o = of_ref[0, rows, cols] + _ret_state_step(qc, kc, vb, wq_ref[:, cols], wk_ref[:, cols],
                                                    gblk_ref[p], sret_ref, p)
        sq = o * o
        ms0 = jnp.sum(jnp.where(head0, sq, 0.0), axis=-1, keepdims=True) * (1.0 / RET_DK)
        ms1 = jnp.sum(jnp.where(head0, 0.0, sq), axis=-1, keepdims=True) * (1.0 / RET_DK)
        o = o * jnp.where(head0, lax.rsqrt(ms0 + EPS), lax.rsqrt(ms1 + EPS))
        gate = _silu(gates_ref[0, rows, cols].astype(F32))
        act_ref[rows, cols] = (o * rnorm_ref[:, cols] * gate).astype(BF16)

    lower = _lower_bound(lb_ref, 1)

    def hgrn_item(h, ci):
        cols = slice(h * LANES, (h + 1) * LANES)
        hcols = slice(RET_WIDTH + h * LANES, RET_WIDTH + (h + 1) * LANES)
        rows = slice(ci * CHUNK, (ci + 1) * CHUNK)

        def load():
            kb, gb = _forget(hfb_ref[0, rows, cols], lower[:, cols])
            q2 = phg_ref[0, rows, cols].astype(F32)
            vi = phg_ref[0, rows, HG_WIDTH + h * LANES:HG_WIDTH + (h + 1) * LANES].astype(F32)
            return q2, kb, gb, vi

        def emit(o_bwd):
            o = of_ref[0, rows, hcols] + o_bwd
            o = o * lax.rsqrt(jnp.mean(o * o, axis=-1, keepdims=True) + EPS)
            gate = _silu(gates_ref[0, rows, hcols].astype(F32))
            act_ref[rows, hcols] = (o * hnorm_ref[:, cols] * gate).astype(BF16)

        return _hgrn_chunk(load, emit, shg_ref, h, tri_ref, code_ref, rev=True)

    chunks = list(reversed(range(nchunk)))
    _interleave([[hgrn_item(h, ci) for h in range(HG_HEADS)] for ci in chunks],
                [functools.partial(ret_item, p, ci) for ci in chunks for p in range(RET_HEADS // 2)])

    up_ret = _dot(act_ref[:, 0:RET_WIDTH], wru_ref[...])
    up_hg = _dot(act_ref[:, RET_WIDTH:], whu_ref[...])
    ga = gates_ref[0, :, 2 * RET_WIDTH:2 * RET_WIDTH + D_MODEL].astype(F32)
    gb_ = gates_ref[0, :, 2 * RET_WIDTH + D_MODEL:].astype(F32)
    merged = (_sigmoid(ga) * up_ret + _sigmoid(gb_) * up_hg).astype(BF16)
    h1_ref[0] = x_ref[0] + _dot(merged, wout_ref[...])


def _gelu_tanh(x):
    return 0.5 * x * (1.0 + jnp.tanh(0.7978845608028654 * (x + 0.044715 * (x * x * x))))


def _ffn_kernel(hin_ref, hres_ref, nffn_ref, win_ref, cw_ref, cb_ref, wdown_ref, nfin_ref,
                y_ref, u_ref, v_ref, act_ref, *, tiles_per_seq):
    j = pl.program_id(0)
    tm = hin_ref.shape[1]
    slot = j % 2
    prev = 1 - slot

    @pl.when(j == 0)
    def _():
        u_ref[...] = jnp.zeros_like(u_ref)
        v_ref[...] = jnp.zeros_like(v_ref)
        act_ref[...] = jnp.zeros_like(act_ref)

    has_left = (j % tiles_per_seq) > 0
    has_right = ((j + tiles_per_seq - 1) % tiles_per_seq) < tiles_per_seq - 1
    hn = _rmsnorm(hin_ref[0], nffn_ref[...]).astype(BF16)

    def project(cb):
        cols = slice(cb * FFN_COLS, (cb + 1) * FFN_COLS)
        u_ref[slot, SUB:SUB + tm, cols] = _dot(hn, win_ref[:, cols])
        v_ref[slot, :, cols] = _dot(hn, win_ref[:, D_FF + cb * FFN_COLS:D_FF + (cb + 1) * FFN_COLS])
        u_ref[prev, SUB + tm:2 * SUB + tm, cols] = jnp.where(has_right, u_ref[slot, SUB:2 * SUB, cols],
                                                             0.0)
        u_ref[slot, 0:SUB, cols] = jnp.where(has_left, u_ref[prev, tm:SUB + tm, cols], 0.0)

    def activate(cb):
        cols = slice(cb * FFN_COLS, (cb + 1) * FFN_COLS)
        uc = (cb_ref[:, cols] + u_ref[prev, SUB - 1:SUB - 1 + tm, cols] * cw_ref[0:1, cols]
              + u_ref[prev, SUB:SUB + tm, cols] * cw_ref[1:2, cols]
              + u_ref[prev, SUB + 1:SUB + 1 + tm, cols] * cw_ref[2:3, cols])
        act_ref[prev, :, cols] = (_gelu_tanh(uc) * v_ref[prev, :, cols]).astype(BF16)

    h2 = []

    def finish():
        cols = slice(len(h2) * FFN_COLS, (len(h2) + 1) * FFN_COLS)
        h2.append(hres_ref[0, :, cols] + _dot(act_ref[slot], wdown_ref[:, cols]))

    n_in, n_out = D_FF // FFN_COLS, D_MODEL // FFN_COLS
    finish_before = {round(i * n_in / n_out) for i in range(n_out)}
    for cb in range(n_in):
        if cb in finish_before:
            finish()
        project(cb)
        activate(cb)
    y_ref[0] = _rmsnorm(jnp.concatenate(h2, axis=1), nfin_ref[...])


def _const_spec(shape):
    zeros = (0,) * len(shape)
    return pl.BlockSpec(shape, lambda *_: zeros, pipeline_mode=pl.Buffered(1))


def _retention_tables(seq_len):
    half = RET_DK // 2
    inv_freq = ROPE_BASE ** (-jnp.arange(half, dtype=F32) / half)
    ang = jnp.arange(seq_len, dtype=F32)[:, None] * inv_freq[None, :]
    cos, sin = jnp.cos(ang), jnp.sin(ang)
    cos_t = jnp.tile(cos, (1, LANES // half))
    sin_t = jnp.tile(jnp.concatenate([-sin, sin], axis=1), (1, LANES // RET_DK))
    log_gamma = jnp.log(1.0 - 2.0 ** (-5.0 - jnp.arange(RET_HEADS, dtype=F32)))
    idx = jnp.arange(CHUNK, dtype=F32)
    dist = jnp.abs(idx[:, None] - idx[None, :])
    dmat = jnp.exp(dist[None] * log_gamma[:, None, None])
    lg_cols = jnp.repeat(log_gamma, RET_DK)[None, :]
    wq_f = jnp.exp((idx + 1.0)[:, None] * lg_cols)
    wk_f = jnp.exp((CHUNK - 1.0 - idx)[:, None] * lg_cols)
    wq_b = jnp.exp((CHUNK - idx)[:, None] * lg_cols)
    wk_b = jnp.exp(idx[:, None] * lg_cols)
    lane_head = jnp.arange(LANES) // RET_DK
    gdec = jnp.exp(CHUNK * log_gamma).reshape(RET_HEADS // 2, 2)
    same = lane_head[:, None] == lane_head[None, :]
    gblk = jnp.where(same[None], gdec[:, lane_head][:, :, None], 0.0)
    return cos_t, sin_t, dmat, wq_f, wk_f, wq_b, wk_b, gblk


def _encode(x, p):
    bsz, seq_len, _ = x.shape
    assert seq_len % TILE == 0 and seq_len % FFN_TILE == 0
    nt = seq_len // TILE
    cos_t, sin_t, dmat, wq_f, wk_f, wq_b, wk_b, gblk = _retention_tables(seq_len)
    tok = jnp.arange(CHUNK)
    tri_f = (tok[None, :] <= tok[:, None]).astype(BF16)
    params = pltpu.CompilerParams(dimension_semantics=("parallel", "arbitrary"),
                                  vmem_limit_bytes=VMEM_LIMIT)

    def tile_spec(width, rev=False):
        if rev:
            return pl.BlockSpec((1, TILE, width), lambda b, j: (b, nt - 1 - j, 0))
        return pl.BlockSpec((1, TILE, width), lambda b, j: (b, j, 0))

    def out(width, dtype):
        return jax.ShapeDtypeStruct((bsz, seq_len, width), dtype)

    state_scratch = [pltpu.VMEM((RET_HEADS // 2, LANES, LANES), F32),
                     pltpu.VMEM((HG_HEADS, LANES, LANES), F32)]

    pret, phg, hfb, gates, o_fwd = pl.pallas_call(
        _mix_fwd_kernel,
        grid=(bsz, nt),
        in_specs=[tile_spec(D_MODEL), _const_spec((1, D_MODEL)), _const_spec((D_MODEL, D_IN)),
                  _const_spec((2, 2, HG_WIDTH)),
                  pl.BlockSpec((TILE, LANES), lambda b, j: (j, 0)),
                  pl.BlockSpec((TILE, LANES), lambda b, j: (j, 0)),
                  _const_spec((RET_HEADS, CHUNK, CHUNK)), _const_spec((CHUNK, RET_WIDTH)),
                  _const_spec((CHUNK, RET_WIDTH)), _const_spec((RET_HEADS // 2, LANES, LANES)),
                  _const_spec((CHUNK, CHUNK)), _const_spec((CHUNK, CHUNK))],
        out_specs=[tile_spec(3 * RET_WIDTH), tile_spec(2 * HG_WIDTH), tile_spec(HG_WIDTH),
                   tile_spec(2 * RET_WIDTH + 2 * D_MODEL), tile_spec(RET_WIDTH + HG_WIDTH)],
        out_shape=[out(3 * RET_WIDTH, BF16), out(2 * HG_WIDTH, BF16), out(HG_WIDTH, F32),
                   out(2 * RET_WIDTH + 2 * D_MODEL, BF16), out(RET_WIDTH + HG_WIDTH, F32)],
        scratch_shapes=state_scratch,
        compiler_params=params,
        name="mix_fwd",
    )(x, p["norm_mix"], p["w_in"], p["lower_bounds"], cos_t, sin_t, dmat, wq_f, wk_f, gblk,
      tri_f, _pair_codes(False))

    h1 = pl.pallas_call(
        _mix_bwd_kernel,
        grid=(bsz, nt),
        in_specs=[tile_spec(D_MODEL, True), tile_spec(3 * RET_WIDTH, True),
                  tile_spec(2 * HG_WIDTH, True), tile_spec(HG_WIDTH, True),
                  tile_spec(2 * RET_WIDTH + 2 * D_MODEL, True), tile_spec(RET_WIDTH + HG_WIDTH, True),
                  _const_spec((2, 2, HG_WIDTH)), _const_spec((CHUNK, RET_WIDTH)),
                  _const_spec((CHUNK, RET_WIDTH)), _const_spec((RET_HEADS // 2, LANES, LANES)),
                  _const_spec((CHUNK, CHUNK)), _const_spec((CHUNK, CHUNK)),
                  _const_spec((1, RET_WIDTH)), _const_spec((1, HG_WIDTH)),
                  _const_spec((RET_WIDTH, D_MODEL)), _const_spec((HG_WIDTH, D_MODEL)),
                  _const_spec((D_MODEL, D_MODEL))],
        out_specs=tile_spec(D_MODEL, True),
        out_shape=out(D_MODEL, F32),
        scratch_shapes=state_scratch + [pltpu.VMEM((TILE, RET_WIDTH + HG_WIDTH), BF16)],
        compiler_params=params,
        name="mix_bwd",
    )(x, pret, phg, hfb, gates, o_fwd, p["lower_bounds"], wq_b, wk_b, gblk, tri_f.T,
      _pair_codes(True),
      p["w_ret_norm"], p["w_hg_norm"], p["w_ret_up"], p["w_hg_up"], p["w_out"])

    nf = seq_len // FFN_TILE
    n_tiles = bsz * nf

    def lagged_tile(lag):
        def index_map(j):
            tile = jnp.clip(j - lag, 0, n_tiles - 1)
            return (tile // nf, tile % nf, 0)
        return pl.BlockSpec((1, FFN_TILE, D_MODEL), index_map)

    return pl.pallas_call(
        functools.partial(_ffn_kernel, tiles_per_seq=nf),
        grid=(n_tiles + 2,),
        in_specs=[lagged_tile(0), lagged_tile(2),
                  _const_spec((1, D_MODEL)), _const_spec((D_MODEL, 2 * D_FF)),
                  _const_spec((3, D_FF)), _const_spec((1, D_FF)), _const_spec((D_FF, D_MODEL)),
                  _const_spec((1, D_MODEL))],
        out_specs=lagged_tile(2),
        out_shape=out(D_MODEL, F32),
        scratch_shapes=[pltpu.VMEM((2, FFN_TILE + 2 * SUB, D_FF), F32),
                        pltpu.VMEM((2, FFN_TILE, D_FF), F32),
                        pltpu.VMEM((2, FFN_TILE, D_FF), BF16)],
        compiler_params=pltpu.CompilerParams(dimension_semantics=("arbitrary",),
                                             vmem_limit_bytes=VMEM_LIMIT),
        name="conv_ffn",
    )(h1, h1, p["norm_ffn"], p["w_ffn_in"], p["conv_w"], p["conv_b"], p["w_ffn_down"],
      p["norm_final"])


def kernel(x_prompt, x_sample, w_in, lower_bounds, w_ret_norm, w_hg_norm, w_ret_up, w_hg_up, w_out,
           norm_mix, norm_ffn, w_ffn_in, conv_w, conv_b, w_ffn_down, norm_final):
    assert w_in.shape[0] == 1, "one layer"
    p = {
        "w_in": w_in[0].astype(BF16),
        "lower_bounds": lower_bounds.astype(F32),
        "w_ret_norm": w_ret_norm.astype(F32),
        "w_hg_norm": w_hg_norm.astype(F32),
        "w_ret_up": w_ret_up[0].astype(BF16),
        "w_hg_up": w_hg_up[0].astype(BF16),
        "w_out": w_out[0].astype(BF16),
        "norm_mix": norm_mix.astype(F32),
        "norm_ffn": norm_ffn.astype(F32),
        "w_ffn_in": w_ffn_in[0].astype(BF16),
        "conv_w": conv_w[0].astype(F32),
        "conv_b": conv_b.astype(F32),
        "w_ffn_down": w_ffn_down[0].astype(BF16),
        "norm_final": norm_final.reshape(1, D_MODEL).astype(F32),
    }
    return _encode(x_prompt, p), _encode(x_sample, p)
```

```python
import functools

import jax
import jax.numpy as jnp
from jax import lax
from jax.experimental import pallas as pl
from jax.experimental.pallas import tpu as pltpu

F32 = jnp.float32
BF16 = jnp.bfloat16

D_MODEL = 1024
RET_HEADS = 8
RET_DK = 64
RET_WIDTH = 512
HG_HEADS = 4
HG_DK = 128
HG_WIDTH = 512
D_FF = 2816
EPS = 1e-6
ROPE_BASE = 10000.0

C_RQ, C_RK, C_RV, C_RG = 0, 512, 1024, 1536
C_HQ, C_HFF, C_HFB, C_HI, C_HG = 2048, 2560, 3072, 3584, 4096
C_GA, C_GB, D_IN = 4608, 5632, 6656

LANES = 128
SUB = 8
CHUNK = 128
TILE = 512
FFN_TILE = 256
FFN_COLS = 256
VMEM_LIMIT = 56 * 1024 * 1024


def _dot(a, b):
    return jnp.dot(a, b, preferred_element_type=F32)


def _dot_nt(a, b):
    return lax.dot_general(a, b, (((1,), (1,)), ((), ())), preferred_element_type=F32)


def _dot_tn(a, b):
    return lax.dot_general(a, b, (((0,), (0,)), ((), ())), preferred_element_type=F32)


def _sigmoid(x):
    return 1.0 / (1.0 + jnp.exp(-x))


def _silu(x):
    return x * _sigmoid(x)


def _rmsnorm(x, w):
    return x * lax.rsqrt(jnp.mean(x * x, axis=-1, keepdims=True) + EPS) * w


def _iota(shape, dim):
    return lax.broadcasted_iota(jnp.int32, shape, dim)


def _lower_bound(lb_ref, direction):
    a0 = lb_ref[0, direction:direction + 1, :]
    a1 = lb_ref[1, direction:direction + 1, :]
    m = jnp.maximum(a0, a1)
    e0 = jnp.exp(a0 - m)
    e1 = jnp.exp(a1 - m)
    return e0 / (e0 + e1)


def _forget(pre, lower):
    f = lower + (1.0 - lower) * _sigmoid(pre)
    return 1.0 - f, jnp.log2(f)


def _rotary(x, cos, sin_signed):
    first_half = (_iota(x.shape, 1) % RET_DK) < (RET_DK // 2)
    swapped = jnp.where(first_half, pltpu.roll(x, LANES - RET_DK // 2, axis=1),
                        pltpu.roll(x, RET_DK // 2, axis=1))
    return x * cos + swapped * sin_signed


def _pair_codes(rev):
    r = jnp.arange(CHUNK, dtype=jnp.int32)[:, None]
    c = jnp.arange(CHUNK, dtype=jnp.int32)[None, :]
    if rev:
        r, c = c, r
    code = jnp.where((r // SUB == c // SUB) & (r >= c), r - c, -1)
    s, level = SUB, 0
    while s < CHUNK:
        sibling = (r // (2 * s) == c // (2 * s)) & (r // s == c // s + 1)
        code = jnp.where(sibling, SUB + level, code)
        s, level = 2 * s, level + 1
    return code


def _hgrn_chunk(load, emit, st_ref, head, tri_ref, code_ref, rev, fillers=(1, 1, 1)):
    q, k, g, v = load()
    c = q.shape[0]
    nsub = c // SUB
    g_hi = g.astype(BF16)
    g_lo = (g - g_hi.astype(F32)).astype(BF16)
    b = _dot(tri_ref[...], g_hi) + _dot(tri_ref[...], g_lo)
    yield fillers[0]

    amat = jnp.where(code_ref[...] == 0, jnp.sum(q * k, axis=-1, keepdims=True), 0.0)
    q3 = q.reshape(nsub, SUB, LANES)
    k3 = k.reshape(nsub, SUB, LANES)
    b3 = b.reshape(nsub, SUB, LANES)
    for d in range(1, SUB):
        shift = (SUB - d) if rev else d
        ks = pltpu.roll(k3, shift, axis=1)
        bs = pltpu.roll(b3, shift, axis=1)
        a = jnp.sum(q3 * ks * jnp.exp2(jnp.minimum(b3 - bs, 0.0)), axis=-1, keepdims=True)
        amat = jnp.where(code_ref[...] == d, a.reshape(c, 1), amat)
    yield fillers[1]

    level_dots = []
    s = SUB
    while s < c:
        nblk = c // s
        if rev:
            edge = [jnp.broadcast_to(b[m * s:m * s + 1, :], (s, LANES)) for m in range(nblk)]
            q_edge = jnp.concatenate(edge[1:] + [jnp.zeros((s, LANES), F32)], axis=0)
        else:
            edge = [jnp.broadcast_to(b[m * s + s - 1:m * s + s, :], (s, LANES)) for m in range(nblk)]
            q_edge = jnp.concatenate([jnp.zeros((s, LANES), F32)] + edge[:-1], axis=0)
        k_edge = jnp.concatenate(edge, axis=0)
        qt = (q * jnp.exp2(b - q_edge)).astype(BF16)
        kt = (k * jnp.exp2(k_edge - b)).astype(BF16)
        level_dots.append(_dot_nt(qt, kt))
        s *= 2

    vb = v.astype(BF16)
    b_tot = b[0:1, :] if rev else b[c - 1:c, :]
    st = st_ref[head]
    o_cross = _dot_nt((q * jnp.exp2(b)).astype(BF16), st.astype(BF16))
    ds_t = _dot_tn(vb, (k * jnp.exp2(b_tot - b)).astype(BF16))
    st_ref[head] = st * jnp.exp2(b_tot) + ds_t
    yield fillers[2]

    for level, a in enumerate(level_dots):
        amat = jnp.where(code_ref[...] == SUB + level, a, amat)
    emit(o_cross + _dot(amat.astype(BF16), vb))


def _interleave(groups, fillers):
    fillers = list(fillers)
    for group in groups:
        active = list(group)
        while active:
            still, want = [], 1
            for gen in active:
                try:
                    want = max(want, next(gen))
                    still.append(gen)
                except StopIteration:
                    pass
            active = still
            for filler in fillers[:want]:
                filler()
            fillers = fillers[want:]
    for filler in fillers:
        filler()


def _ret_state_step(q, k, vb, wq, wk, gblk, st_ref, pair):
    st = st_ref[pair]
    o = _dot((q * wq).astype(BF16), st.astype(BF16))
    ds = _dot_tn((k * wk).astype(BF16), vb)
    same_head = (_iota((LANES, LANES), 0) < RET_DK) == (_iota((LANES, LANES), 1) < RET_DK)
    st_ref[pair] = st * gblk + jnp.where(same_head, ds, 0.0)
    return o


def _mix_fwd_kernel(x_ref, nmix_ref, w_ref, lb_ref, cos_ref, sin_ref, dmat_ref, wq_ref, wk_ref,
                    gblk_ref, tri_ref, code_ref, pret_ref, phg_ref, hfb_ref, of_ref,
                    sret_ref, shg_ref):
    @pl.when(pl.program_id(1) == 0)
    def _():
        sret_ref[...] = jnp.zeros_like(sret_ref)
        shg_ref[...] = jnp.zeros_like(shg_ref)

    tm = x_ref.shape[1]
    nchunk = tm // CHUNK
    hn = _rmsnorm(x_ref[0], nmix_ref[...]).astype(BF16)

    def proj(lo, width):
        return _dot(hn, w_ref[:, lo:lo + width])

    q2 = _silu(proj(C_HQ, HG_WIDTH))
    vi = proj(C_HI, HG_WIDTH)
    phg_ref[0, :, 0:HG_WIDTH] = q2.astype(BF16)
    phg_ref[0, :, HG_WIDTH:] = vi.astype(BF16)
    kf, gf = _forget(proj(C_HFF, HG_WIDTH), _lower_bound(lb_ref, 0))

    def hgrn_item(h, ci):
        cols = slice(h * LANES, (h + 1) * LANES)
        rows = slice(ci * CHUNK, (ci + 1) * CHUNK)

        def load():
            return q2[rows, cols], kf[rows, cols], gf[rows, cols], vi[rows, cols]

        def emit(o):
            of_ref[0, rows, RET_WIDTH + h * LANES:RET_WIDTH + (h + 1) * LANES] = o

        return _hgrn_chunk(load, emit, shg_ref, h, tri_ref, code_ref, rev=False, fillers=(1, 2, 1))

    piece = 2 * LANES

    def hfb_item(i):
        hfb_ref[0, :, i * piece:(i + 1) * piece] = proj(C_HFB + i * piece, piece)

    ret_qkv = {}

    def ret_proj_item(group):
        pq = proj(C_RQ + group * piece, piece)
        pk = proj(C_RK + group * piece, piece)
        pv = proj(C_RV + group * piece, piece)
        cos = cos_ref[...]
        sin = sin_ref[...]
        for pi in range(2):
            p = 2 * group + pi
            lcols = slice(pi * LANES, (pi + 1) * LANES)
            q = _rotary(pq[:, lcols], cos, sin)
            k = _rotary(pk[:, lcols], cos, sin) * (RET_DK ** -0.5)
            v = pv[:, lcols]
            pret_ref[0, :, p * LANES:(p + 1) * LANES] = q.astype(BF16)
            pret_ref[0, :, RET_WIDTH + p * LANES:RET_WIDTH + (p + 1) * LANES] = k.astype(BF16)
            pret_ref[0, :, 2 * RET_WIDTH + p * LANES:2 * RET_WIDTH + (p + 1) * LANES] = v.astype(BF16)
            ret_qkv[p] = (q, k, v)

    def ret_pair_item(p, ci):
        q, k, v = ret_qkv[p]
        cols = slice(p * LANES, (p + 1) * LANES)
        head0 = _iota((CHUNK, LANES), 1) < RET_DK
        rows = slice(ci * CHUNK, (ci + 1) * CHUNK)
        qc, kc, vc = q[rows], k[rows], v[rows]
        kb = kc.astype(BF16)
        vb = vc.astype(BF16)
        s0 = _dot_nt(jnp.where(head0, qc, 0.0).astype(BF16), kb) * dmat_ref[2 * p]
        s1 = _dot_nt(jnp.where(head0, 0.0, qc).astype(BF16), kb) * dmat_ref[2 * p + 1]
        pcat = jnp.concatenate([s0.astype(BF16), s1.astype(BF16)], axis=1)
        zero = jnp.zeros_like(vb)
        vblk = jnp.concatenate([jnp.where(head0, vb, zero), jnp.where(head0, zero, vb)], axis=0)
        o = _dot(pcat, vblk)
        o = o + _ret_state_step(qc, kc, vb, wq_ref[:, cols], wk_ref[:, cols], gblk_ref[p],
                                sret_ref, p)
        of_ref[0, rows, cols] = o

    mxu_items = []
    for group in range(RET_HEADS // 4):
        mxu_items.append(functools.partial(ret_proj_item, group))
        mxu_items += [functools.partial(ret_pair_item, p, ci)
                      for p in (2 * group, 2 * group + 1) for ci in range(nchunk)]
        mxu_items.append(functools.partial(hfb_item, group))
    _interleave([[hgrn_item(h, ci) for h in range(HG_HEADS)] for ci in range(nchunk)], mxu_items)


def _mix_bwd_kernel(x_ref, pret_ref, phg_ref, hfb_ref, of_ref, nmix_ref, wg_ref, lb_ref, wq_ref,
                    wk_ref, gblk_ref, tri_ref, code_ref, rnorm_ref, hnorm_ref, wru_ref, whu_ref,
                    wout_ref, h1_ref, sret_ref, shg_ref, act_ref, gate_ref):
    @pl.when(pl.program_id(1) == 0)
    def _():
        sret_ref[...] = jnp.zeros_like(sret_ref)
        shg_ref[...] = jnp.zeros_like(shg_ref)

    tm = x_ref.shape[1]
    nchunk = tm // CHUNK
    head0 = _iota((CHUNK, LANES), 1) < RET_DK
    hn = _rmsnorm(x_ref[0], nmix_ref[...]).astype(BF16)
    piece = 2 * LANES

    def gate_item(i):
        cols = slice(i * piece, (i + 1) * piece)
        pre = _dot(hn, wg_ref[:, cols])
        gate_ref[:, cols] = _silu(pre) if (i + 1) * piece <= RET_WIDTH + HG_WIDTH else _sigmoid(pre)

    def ret_item(p, ci):
        cols = slice(p * LANES, (p + 1) * LANES)
        rows = slice(ci * CHUNK, (ci + 1) * CHUNK)
        qc = pret_ref[0, rows, p * LANES:(p + 1) * LANES].astype(F32)
        kc = pret_ref[0, rows, RET_WIDTH + p * LANES:RET_WIDTH + (p + 1) * LANES].astype(F32)
        vb = pret_ref[0, rows, 2 * RET_WIDTH + p * LANES:2 * RET_WIDTH + (p + 1) * LANES]
        o = of_ref[0, rows, cols] + _ret_state_step(qc, kc, vb, wq_ref[:, cols], wk_ref[:, cols],
                                                    gblk_ref[p], sret_ref, p)
        sq = o * o
        ms0 = jnp.sum(jnp.where(head0, sq, 0.0), axis=-1, keepdims=True) * (1.0 / RET_DK)
        ms1 = jnp.sum(jnp.where(head0, 0.0, sq), axis=-1, keepdims=True) * (1.0 / RET_DK)
        o = o * jnp.where(head0, lax.rsqrt(ms0 + EPS), lax.rsqrt(ms1 + EPS))
        act_ref[rows, cols] = (o * rnorm_ref[:, cols] * gate_ref[rows, cols]).astype(BF16)

    lower = _lower_bound(lb_ref, 1)

    def hgrn_item(h, ci):
        cols = slice(h * LANES, (h + 1) * LANES)
        hcols = slice(RET_WIDTH + h * LANES, RET_WIDTH + (h + 1) * LANES)
        rows = slice(ci * CHUNK, (ci + 1) * CHUNK)

        def load():
            kb, gb = _forget(hfb_ref[0, rows, cols], lower[:, cols])
            q2 = phg_ref[0, rows, cols].astype(F32)
            vi = phg_ref[0, rows, HG_WIDTH + h * LANES:HG_WIDTH + (h + 1) * LANES].astype(F32)
            return q2, kb, gb, vi

        def emit(o_bwd):
            o = of_ref[0, rows, hcols] + o_bwd
            o = o * lax.rsqrt(jnp.mean(o * o, axis=-1, keepdims=True) + EPS)
            act_ref[rows, hcols] = (o * hnorm_ref[:, cols] * gate_ref[rows, hcols]).astype(BF16)

        return _hgrn_chunk(load, emit, shg_ref, h, tri_ref, code_ref, rev=True, fillers=(1, 5, 2))

    chunks = list(reversed(range(nchunk)))
    n_out_gates = (RET_WIDTH + HG_WIDTH) // piece
    ret_items = [functools.partial(ret_item, p, ci) for ci in chunks for p in range(RET_HEADS // 2)]
    gate_items = [functools.partial(gate_item, i) for i in range(gate_ref.shape[1] // piece)]
    _interleave([[hgrn_item(h, ci) for h in range(HG_HEADS)] for ci in chunks],
                gate_items[:n_out_gates] + ret_items + gate_items[n_out_gates:])

    up_ret = _dot(act_ref[:, 0:RET_WIDTH], wru_ref[...])
    up_hg = _dot(act_ref[:, RET_WIDTH:], whu_ref[...])
    ga = gate_ref[:, 2 * RET_WIDTH:2 * RET_WIDTH + D_MODEL]
    gb_ = gate_ref[:, 2 * RET_WIDTH + D_MODEL:]
    merged = (ga * up_ret + gb_ * up_hg).astype(BF16)
    h1_ref[0] = x_ref[0] + _dot(merged, wout_ref[...])


def _gelu_tanh(x):
    return 0.5 * x * (1.0 + jnp.tanh(0.7978845608028654 * (x + 0.044715 * (x * x * x))))


def _ffn_kernel(hin_ref, hres_ref, nffn_ref, win_ref, cw_ref, cb_ref, wdown_ref, nfin_ref,
                y_ref, u_ref, v_ref, act_ref, *, tiles_per_seq):
    j = pl.program_id(0)
    tm = hin_ref.shape[1]
    slot = j % 2
    prev = 1 - slot

    @pl.when(j == 0)
    def _():
        u_ref[...] = jnp.zeros_like(u_ref)
        v_ref[...] = jnp.zeros_like(v_ref)
        act_ref[...] = jnp.zeros_like(act_ref)

    has_left = (j % tiles_per_seq) > 0
    has_right = ((j + tiles_per_seq - 1) % tiles_per_seq) < tiles_per_seq - 1
    hn = _rmsnorm(hin_ref[0], nffn_ref[...]).astype(BF16)

    def project(cb):
        cols = slice(cb * FFN_COLS, (cb + 1) * FFN_COLS)
        u_ref[slot, SUB:SUB + tm, cols] = _dot(hn, win_ref[:, cols])
        v_ref[slot, :, cols] = _dot(hn, win_ref[:, D_FF + cb * FFN_COLS:D_FF + (cb + 1) * FFN_COLS])
        u_ref[prev, SUB + tm:2 * SUB + tm, cols] = jnp.where(has_right, u_ref[slot, SUB:2 * SUB, cols],
                                                             0.0)
        u_ref[slot, 0:SUB, cols] = jnp.where(has_left, u_ref[prev, tm:SUB + tm, cols], 0.0)

    def activate(cb):
        cols = slice(cb * FFN_COLS, (cb + 1) * FFN_COLS)
        uc = (cb_ref[:, cols] + u_ref[prev, SUB - 1:SUB - 1 + tm, cols] * cw_ref[0:1, cols]
              + u_ref[prev, SUB:SUB + tm, cols] * cw_ref[1:2, cols]
              + u_ref[prev, SUB + 1:SUB + 1 + tm, cols] * cw_ref[2:3, cols])
        act_ref[prev, :, cols] = (_gelu_tanh(uc) * v_ref[prev, :, cols]).astype(BF16)

    h2 = []

    def finish():
        cols = slice(len(h2) * FFN_COLS, (len(h2) + 1) * FFN_COLS)
        h2.append(hres_ref[0, :, cols] + _dot(act_ref[slot], wdown_ref[:, cols]))

    n_in, n_out = D_FF // FFN_COLS, D_MODEL // FFN_COLS
    finish_before = {round(i * n_in / n_out) for i in range(n_out)}
    for cb in range(n_in):
        if cb in finish_before:
            finish()
        project(cb)
        activate(cb)
    y_ref[0] = _rmsnorm(jnp.concatenate(h2, axis=1), nfin_ref[...])


def _const_spec(shape):
    zeros = (0,) * len(shape)
    return pl.BlockSpec(shape, lambda *_: zeros, pipeline_mode=pl.Buffered(1))


def _retention_tables(seq_len):
    half = RET_DK // 2
    inv_freq = ROPE_BASE ** (-jnp.arange(half, dtype=F32) / half)
    ang = jnp.arange(seq_len, dtype=F32)[:, None] * inv_freq[None, :]
    cos, sin = jnp.cos(ang), jnp.sin(ang)
    cos_t = jnp.tile(cos, (1, LANES // half))
    sin_t = jnp.tile(jnp.concatenate([-sin, sin], axis=1), (1, LANES // RET_DK))
    log_gamma = jnp.log(1.0 - 2.0 ** (-5.0 - jnp.arange(RET_HEADS, dtype=F32)))
    idx = jnp.arange(CHUNK, dtype=F32)
    dist = jnp.abs(idx[:, None] - idx[None, :])
    dmat = jnp.exp(dist[None] * log_gamma[:, None, None])
    lg_cols = jnp.repeat(log_gamma, RET_DK)[None, :]
    wq_f = jnp.exp((idx + 1.0)[:, None] * lg_cols)
    wk_f = jnp.exp((CHUNK - 1.0 - idx)[:, None] * lg_cols)
    wq_b = jnp.exp((CHUNK - idx)[:, None] * lg_cols)
    wk_b = jnp.exp(idx[:, None] * lg_cols)
    lane_head = jnp.arange(LANES) // RET_DK
    gdec = jnp.exp(CHUNK * log_gamma).reshape(RET_HEADS // 2, 2)
    same = lane_head[:, None] == lane_head[None, :]
    gblk = jnp.where(same[None], gdec[:, lane_head][:, :, None], 0.0)
    return cos_t, sin_t, dmat, wq_f, wk_f, wq_b, wk_b, gblk


def _encode(x, p):
    bsz, seq_len, _ = x.shape
    assert seq_len % TILE == 0 and seq_len % FFN_TILE == 0
    nt = seq_len // TILE
    cos_t, sin_t, dmat, wq_f, wk_f, wq_b, wk_b, gblk = _retention_tables(seq_len)
    tok = jnp.arange(CHUNK)
    tri_f = (tok[None, :] <= tok[:, None]).astype(BF16)
    params = pltpu.CompilerParams(dimension_semantics=("parallel", "arbitrary"),
                                  vmem_limit_bytes=VMEM_LIMIT)

    def tile_spec(width, rev=False):
        if rev:
            return pl.BlockSpec((1, TILE, width), lambda b, j: (b, nt - 1 - j, 0))
        return pl.BlockSpec((1, TILE, width), lambda b, j: (b, j, 0))

    def out(width, dtype):
        return jax.ShapeDtypeStruct((bsz, seq_len, width), dtype)

    state_scratch = [pltpu.VMEM((RET_HEADS // 2, LANES, LANES), F32),
                     pltpu.VMEM((HG_HEADS, LANES, LANES), F32)]

    n_gate = 2 * RET_WIDTH + 2 * D_MODEL
    pret, phg, hfb, o_fwd = pl.pallas_call(
        _mix_fwd_kernel,
        grid=(bsz, nt),
        in_specs=[tile_spec(D_MODEL), _const_spec((1, D_MODEL)), _const_spec((D_MODEL, D_IN)),
                  _const_spec((2, 2, HG_WIDTH)),
                  pl.BlockSpec((TILE, LANES), lambda b, j: (j, 0)),
                  pl.BlockSpec((TILE, LANES), lambda b, j: (j, 0)),
                  _const_spec((RET_HEADS, CHUNK, CHUNK)), _const_spec((CHUNK, RET_WIDTH)),
                  _const_spec((CHUNK, RET_WIDTH)), _const_spec((RET_HEADS // 2, LANES, LANES)),
                  _const_spec((CHUNK, CHUNK)), _const_spec((CHUNK, CHUNK))],
        out_specs=[tile_spec(3 * RET_WIDTH), tile_spec(2 * HG_WIDTH), tile_spec(HG_WIDTH),
                   tile_spec(RET_WIDTH + HG_WIDTH)],
        out_shape=[out(3 * RET_WIDTH, BF16), out(2 * HG_WIDTH, BF16), out(HG_WIDTH, F32),
                   out(RET_WIDTH + HG_WIDTH, F32)],
        scratch_shapes=state_scratch,
        compiler_params=params,
        name="mix_fwd",
    )(x, p["norm_mix"], p["w_in"], p["lower_bounds"], cos_t, sin_t, dmat, wq_f, wk_f, gblk,
      tri_f, _pair_codes(False))

    h1 = pl.pallas_call(
        _mix_bwd_kernel,
        grid=(bsz, nt),
        in_specs=[tile_spec(D_MODEL, True), tile_spec(3 * RET_WIDTH, True),
                  tile_spec(2 * HG_WIDTH, True), tile_spec(HG_WIDTH, True),
                  tile_spec(RET_WIDTH + HG_WIDTH, True),
                  _const_spec((1, D_MODEL)), _const_spec((D_MODEL, n_gate)),
                  _const_spec((2, 2, HG_WIDTH)), _const_spec((CHUNK, RET_WIDTH)),
                  _const_spec((CHUNK, RET_WIDTH)), _const_spec((RET_HEADS // 2, LANES, LANES)),
                  _const_spec((CHUNK, CHUNK)), _const_spec((CHUNK, CHUNK)),
                  _const_spec((1, RET_WIDTH)), _const_spec((1, HG_WIDTH)),
                  _const_spec((RET_WIDTH, D_MODEL)), _const_spec((HG_WIDTH, D_MODEL)),
                  _const_spec((D_MODEL, D_MODEL))],
        out_specs=tile_spec(D_MODEL, True),
        out_shape=out(D_MODEL, F32),
        scratch_shapes=state_scratch + [pltpu.VMEM((TILE, RET_WIDTH + HG_WIDTH), BF16),
                                        pltpu.VMEM((TILE, n_gate), F32)],
        compiler_params=params,
        name="mix_bwd",
    )(x, pret, phg, hfb, o_fwd, p["norm_mix"], p["w_gates"], p["lower_bounds"], wq_b, wk_b, gblk,
      tri_f.T, _pair_codes(True),
      p["w_ret_norm"], p["w_hg_norm"], p["w_ret_up"], p["w_hg_up"], p["w_out"])

    nf = seq_len // FFN_TILE
    n_tiles = bsz * nf

    def lagged_tile(lag):
        def index_map(j):
            tile = jnp.clip(j - lag, 0, n_tiles - 1)
            return (tile // nf, tile % nf, 0)
        return pl.BlockSpec((1, FFN_TILE, D_MODEL), index_map)

    return pl.pallas_call(
        functools.partial(_ffn_kernel, tiles_per_seq=nf),
        grid=(n_tiles + 2,),
        in_specs=[lagged_tile(0), lagged_tile(2),
                  _const_spec((1, D_MODEL)), _const_spec((D_MODEL, 2 * D_FF)),
                  _const_spec((3, D_FF)), _const_spec((1, D_FF)), _const_spec((D_FF, D_MODEL)),
                  _const_spec((1, D_MODEL))],
        out_specs=lagged_tile(2),
        out_shape=out(D_MODEL, F32),
        scratch_shapes=[pltpu.VMEM((2, FFN_TILE + 2 * SUB, D_FF), F32),
                        pltpu.VMEM((2, FFN_TILE, D_FF), F32),
                        pltpu.VMEM((2, FFN_TILE, D_FF), BF16)],
        compiler_params=pltpu.CompilerParams(dimension_semantics=("arbitrary",),
                                             vmem_limit_bytes=VMEM_LIMIT),
        name="conv_ffn",
    )(h1, h1, p["norm_ffn"], p["w_ffn_in"], p["conv_w"], p["conv_b"], p["w_ffn_down"],
      p["norm_final"])


def kernel(x_prompt, x_sample, w_in, lower_bounds, w_ret_norm, w_hg_norm, w_ret_up, w_hg_up, w_out,
           norm_mix, norm_ffn, w_ffn_in, conv_w, conv_b, w_ffn_down, norm_final):
    assert w_in.shape[0] == 1, "one layer"
    p = {
        "w_in": w_in[0].astype(BF16),
        "w_gates": jnp.concatenate([w_in[0, :, C_RG:C_RG + RET_WIDTH], w_in[0, :, C_HG:]],
                                   axis=1).astype(BF16),
        "lower_bounds": lower_bounds.astype(F32),
        "w_ret_norm": w_ret_norm.astype(F32),
        "w_hg_norm": w_hg_norm.astype(F32),
        "w_ret_up": w_ret_up[0].astype(BF16),
        "w_hg_up": w_hg_up[0].astype(BF16),
        "w_out": w_out[0].astype(BF16),
        "norm_mix": norm_mix.astype(F32),
        "norm_ffn": norm_ffn.astype(F32),
        "w_ffn_in": w_ffn_in[0].astype(BF16),
        "conv_w": conv_w[0].astype(F32),
        "conv_b": conv_b.astype(F32),
        "w_ffn_down": w_ffn_down[0].astype(BF16),
        "norm_final": norm_final.reshape(1, D_MODEL).astype(F32),
    }
    return _encode(x_prompt, p), _encode(x_sample, p)
```

```python
import functools

import jax
import jax.numpy as jnp
from jax import lax
from jax.experimental import pallas as pl
from jax.experimental.pallas import tpu as pltpu

F32 = jnp.float32
BF16 = jnp.bfloat16

D_MODEL = 1024
RET_HEADS = 8
RET_DK = 64
RET_WIDTH = 512
HG_HEADS = 4
HG_DK = 128
HG_WIDTH = 512
D_FF = 2816
EPS = 1e-6
ROPE_BASE = 10000.0

C_RQ, C_RK, C_RV, C_RG = 0, 512, 1024, 1536
C_HQ, C_HFF, C_HFB, C_HI, C_HG = 2048, 2560, 3072, 3584, 4096
C_GA, C_GB, D_IN = 4608, 5632, 6656

LANES = 128
SUB = 8
CHUNK = 128
TILE = 512
FFN_TILE = 256
FFN_COLS = 256
VMEM_LIMIT = 56 * 1024 * 1024


def _dot(a, b):
    return jnp.dot(a, b, preferred_element_type=F32)


def _dot_nt(a, b):
    return lax.dot_general(a, b, (((1,), (1,)), ((), ())), preferred_element_type=F32)


def _dot_tn(a, b):
    return lax.dot_general(a, b, (((0,), (0,)), ((), ())), preferred_element_type=F32)


def _sigmoid(x):
    return 1.0 / (1.0 + jnp.exp(-x))


def _silu(x):
    return x * _sigmoid(x)


def _rmsnorm(x, w):
    return x * lax.rsqrt(jnp.mean(x * x, axis=-1, keepdims=True) + EPS) * w


def _iota(shape, dim):
    return lax.broadcasted_iota(jnp.int32, shape, dim)


def _lower_bound(lb_ref, direction):
    a0 = lb_ref[0, direction:direction + 1, :]
    a1 = lb_ref[1, direction:direction + 1, :]
    m = jnp.maximum(a0, a1)
    e0 = jnp.exp(a0 - m)
    e1 = jnp.exp(a1 - m)
    return e0 / (e0 + e1)


def _forget(pre, lower):
    f = lower + (1.0 - lower) * _sigmoid(pre)
    return 1.0 - f, jnp.log2(f)


def _rotary(x, cos, sin_signed):
    first_half = (_iota(x.shape, 1) % RET_DK) < (RET_DK // 2)
    swapped = jnp.where(first_half, pltpu.roll(x, LANES - RET_DK // 2, axis=1),
                        pltpu.roll(x, RET_DK // 2, axis=1))
    return x * cos + swapped * sin_signed


def _pair_codes(rev):
    r = jnp.arange(CHUNK, dtype=jnp.int32)[:, None]
    c = jnp.arange(CHUNK, dtype=jnp.int32)[None, :]
    if rev:
        r, c = c, r
    code = jnp.where((r // SUB == c // SUB) & (r >= c), r - c, -1)
    s, level = SUB, 0
    while s < CHUNK:
        sibling = (r // (2 * s) == c // (2 * s)) & (r // s == c // s + 1)
        code = jnp.where(sibling, SUB + level, code)
        s, level = 2 * s, level + 1
    return code


def _hgrn_chunk(load, emit, st_ref, head, tri_ref, code_ref, rev, fillers=(1, 1, 1)):
    q, k, g, v = load()
    c = q.shape[0]
    nsub = c // SUB
    g_hi = g.astype(BF16)
    g_lo = (g - g_hi.astype(F32)).astype(BF16)
    b = _dot(tri_ref[...], g_hi) + _dot(tri_ref[...], g_lo)
    yield fillers[0]

    amat = jnp.where(code_ref[...] == 0, jnp.sum(q * k, axis=-1, keepdims=True), 0.0)
    q3 = q.reshape(nsub, SUB, LANES)
    k3 = k.reshape(nsub, SUB, LANES)
    b3 = b.reshape(nsub, SUB, LANES)
    for d in range(1, SUB):
        shift = (SUB - d) if rev else d
        ks = pltpu.roll(k3, shift, axis=1)
        bs = pltpu.roll(b3, shift, axis=1)
        a = jnp.sum(q3 * ks * jnp.exp2(b3 - bs), axis=-1, keepdims=True)
        amat = jnp.where(code_ref[...] == d, a.reshape(c, 1), amat)
    yield fillers[1]

    level_dots = []
    s = SUB
    while s < c:
        nblk = c // s
        if rev:
            edge = [jnp.broadcast_to(b[m * s:m * s + 1, :], (s, LANES)) for m in range(nblk)]
            q_edge = jnp.concatenate(edge[1:] + [jnp.zeros((s, LANES), F32)], axis=0)
        else:
            edge = [jnp.broadcast_to(b[m * s + s - 1:m * s + s, :], (s, LANES)) for m in range(nblk)]
            q_edge = jnp.concatenate([jnp.zeros((s, LANES), F32)] + edge[:-1], axis=0)
        k_edge = jnp.concatenate(edge, axis=0)
        qt = (q * jnp.exp2(b - q_edge)).astype(BF16)
        kt = (k * jnp.exp2(k_edge - b)).astype(BF16)
        level_dots.append(_dot_nt(qt, kt))
        s *= 2

    vb = v.astype(BF16)
    b_tot = b[0:1, :] if rev else b[c - 1:c, :]
    st = st_ref[head]
    o_cross = _dot_nt((q * jnp.exp2(b)).astype(BF16), st.astype(BF16))
    ds_t = _dot_tn(vb, (k * jnp.exp2(b_tot - b)).astype(BF16))
    st_ref[head] = st * jnp.exp2(b_tot) + ds_t
    yield fillers[2]

    for level, a in enumerate(level_dots):
        amat = jnp.where(code_ref[...] == SUB + level, a, amat)
    emit(o_cross + _dot(amat.astype(BF16), vb))


def _interleave(groups, fillers):
    fillers = list(fillers)
    for group in groups:
        active = list(group)
        while active:
            still, want = [], 1
            for gen in active:
                try:
                    want = max(want, next(gen))
                    still.append(gen)
                except StopIteration:
                    pass
            active = still
            for filler in fillers[:want]:
                filler()
            fillers = fillers[want:]
    for filler in fillers:
        filler()


def _ret_state_step(q, k, vb, wq, wk, gblk, st_ref, pair):
    st = st_ref[pair]
    o = _dot((q * wq).astype(BF16), st.astype(BF16))
    ds = _dot_tn((k * wk).astype(BF16), vb)
    same_head = (_iota((LANES, LANES), 0) < RET_DK) == (_iota((LANES, LANES), 1) < RET_DK)
    st_ref[pair] = st * gblk + jnp.where(same_head, ds, 0.0)
    return o


def _mix_fwd_kernel(x_ref, nmix_ref, w_ref, lb_ref, cos_ref, sin_ref, dmat_ref, wq_ref, wk_ref,
                    gblk_ref, tri_ref, code_ref, pret_ref, phg_ref, hfb_ref, of_ref,
                    sret_ref, shg_ref):
    @pl.when(pl.program_id(1) == 0)
    def _():
        sret_ref[...] = jnp.zeros_like(sret_ref)
        shg_ref[...] = jnp.zeros_like(shg_ref)

    tm = x_ref.shape[1]
    nchunk = tm // CHUNK
    hn = _rmsnorm(x_ref[0], nmix_ref[...]).astype(BF16)

    def proj(lo, width):
        return _dot(hn, w_ref[:, lo:lo + width])

    q2 = _silu(proj(C_HQ, HG_WIDTH))
    vi = proj(C_HI, HG_WIDTH)
    phg_ref[0, :, 0:HG_WIDTH] = q2.astype(BF16)
    phg_ref[0, :, HG_WIDTH:] = vi.astype(BF16)
    kf, gf = _forget(proj(C_HFF, HG_WIDTH), _lower_bound(lb_ref, 0))

    def hgrn_item(h, ci):
        cols = slice(h * LANES, (h + 1) * LANES)
        rows = slice(ci * CHUNK, (ci + 1) * CHUNK)

        def load():
            return q2[rows, cols], kf[rows, cols], gf[rows, cols], vi[rows, cols]

        def emit(o):
            of_ref[0, rows, RET_WIDTH + h * LANES:RET_WIDTH + (h + 1) * LANES] = o

        return _hgrn_chunk(load, emit, shg_ref, h, tri_ref, code_ref, rev=False, fillers=(2, 2, 1))

    piece = 2 * LANES

    def hfb_item(i):
        hfb_ref[0, :, i * piece:(i + 1) * piece] = proj(C_HFB + i * piece, piece)

    ret_qkv = {}

    def ret_proj_item(group):
        pq = proj(C_RQ + group * piece, piece)
        pk = proj(C_RK + group * piece, piece)
        pv = proj(C_RV + group * piece, piece)
        cos = cos_ref[...]
        sin = sin_ref[...]
        for pi in range(2):
            p = 2 * group + pi
            lcols = slice(pi * LANES, (pi + 1) * LANES)
            q = _rotary(pq[:, lcols], cos, sin)
            k = _rotary(pk[:, lcols], cos, sin) * (RET_DK ** -0.5)
            v = pv[:, lcols]
            pret_ref[0, :, p * LANES:(p + 1) * LANES] = q.astype(BF16)
            pret_ref[0, :, RET_WIDTH + p * LANES:RET_WIDTH + (p + 1) * LANES] = k.astype(BF16)
            pret_ref[0, :, 2 * RET_WIDTH + p * LANES:2 * RET_WIDTH + (p + 1) * LANES] = v.astype(BF16)
            ret_qkv[p] = (q, k, v)

    def ret_pair_item(p, ci):
        q, k, v = ret_qkv[p]
        cols = slice(p * LANES, (p + 1) * LANES)
        head0 = _iota((CHUNK, LANES), 1) < RET_DK
        rows = slice(ci * CHUNK, (ci + 1) * CHUNK)
        qc, kc, vc = q[rows], k[rows], v[rows]
        kb = kc.astype(BF16)
        vb = vc.astype(BF16)
        s0 = _dot_nt(jnp.where(head0, qc, 0.0).astype(BF16), kb) * dmat_ref[2 * p]
        s1 = _dot_nt(jnp.where(head0, 0.0, qc).astype(BF16), kb) * dmat_ref[2 * p + 1]
        pcat = jnp.concatenate([s0.astype(BF16), s1.astype(BF16)], axis=1)
        zero = jnp.zeros_like(vb)
        vblk = jnp.concatenate([jnp.where(head0, vb, zero), jnp.where(head0, zero, vb)], axis=0)
        o = _dot(pcat, vblk)
        o = o + _ret_state_step(qc, kc, vb, wq_ref[:, cols], wk_ref[:, cols], gblk_ref[p],
                                sret_ref, p)
        of_ref[0, rows, cols] = o

    mxu_items = []
    for group in range(RET_HEADS // 4):
        mxu_items.append(functools.partial(ret_proj_item, group))
        mxu_items += [functools.partial(ret_pair_item, p, ci)
                      for p in (2 * group, 2 * group + 1) for ci in range(nchunk)]
        mxu_items.append(functools.partial(hfb_item, group))
    _interleave([[hgrn_item(h, ci) for h in range(HG_HEADS)] for ci in range(nchunk)], mxu_items)


def _mix_bwd_kernel(x_ref, pret_ref, phg_ref, hfb_ref, of_ref, nmix_ref, wg_ref, lb_ref, wq_ref,
                    wk_ref, gblk_ref, tri_ref, code_ref, rnorm_ref, hnorm_ref, wru_ref, whu_ref,
                    wout_ref, h1_ref, sret_ref, shg_ref, act_ref, gate_ref):
    @pl.when(pl.program_id(1) == 0)
    def _():
        sret_ref[...] = jnp.zeros_like(sret_ref)
        shg_ref[...] = jnp.zeros_like(shg_ref)

    tm = x_ref.shape[1]
    nchunk = tm // CHUNK
    head0 = _iota((CHUNK, LANES), 1) < RET_DK
    hn = _rmsnorm(x_ref[0], nmix_ref[...]).astype(BF16)
    piece = 2 * LANES

    def gate_item(i):
        cols = slice(i * piece, (i + 1) * piece)
        pre = _dot(hn, wg_ref[:, cols])
        gate_ref[:, cols] = _silu(pre) if (i + 1) * piece <= RET_WIDTH + HG_WIDTH else _sigmoid(pre)

    def ret_item(p, ci):
        cols = slice(p * LANES, (p + 1) * LANES)
        rows = slice(ci * CHUNK, (ci + 1) * CHUNK)
        qc = pret_ref[0, rows, p * LANES:(p + 1) * LANES].astype(F32)
        kc = pret_ref[0, rows, RET_WIDTH + p * LANES:RET_WIDTH + (p + 1) * LANES].astype(F32)
        vb = pret_ref[0, rows, 2 * RET_WIDTH + p * LANES:2 * RET_WIDTH + (p + 1) * LANES]
        o = of_ref[0, rows, cols] + _ret_state_step(qc, kc, vb, wq_ref[:, cols], wk_ref[:, cols],
                                                    gblk_ref[p], sret_ref, p)
        sq = o * o
        ms0 = jnp.sum(jnp.where(head0, sq, 0.0), axis=-1, keepdims=True) * (1.0 / RET_DK)
        ms1 = jnp.sum(jnp.where(head0, 0.0, sq), axis=-1, keepdims=True) * (1.0 / RET_DK)
        o = o * jnp.where(head0, lax.rsqrt(ms0 + EPS), lax.rsqrt(ms1 + EPS))
        act_ref[rows, cols] = (o * rnorm_ref[:, cols] * gate_ref[rows, cols]).astype(BF16)

    lower = _lower_bound(lb_ref, 1)

    def hgrn_item(h, ci):
        cols = slice(h * LANES, (h + 1) * LANES)
        hcols = slice(RET_WIDTH + h * LANES, RET_WIDTH + (h + 1) * LANES)
        rows = slice(ci * CHUNK, (ci + 1) * CHUNK)

        def load():
            kb, gb = _forget(hfb_ref[0, rows, cols], lower[:, cols])
            q2 = phg_ref[0, rows, cols].astype(F32)
            vi = phg_ref[0, rows, HG_WIDTH + h * LANES:HG_WIDTH + (h + 1) * LANES].astype(F32)
            return q2, kb, gb, vi

        def emit(o_bwd):
            o = of_ref[0, rows, hcols] + o_bwd
            o = o * lax.rsqrt(jnp.mean(o * o, axis=-1, keepdims=True) + EPS)
            act_ref[rows, hcols] = (o * hnorm_ref[:, cols] * gate_ref[rows, hcols]).astype(BF16)

        return _hgrn_chunk(load, emit, shg_ref, h, tri_ref, code_ref, rev=True, fillers=(2, 3, 2))

    chunks = list(reversed(range(nchunk)))
    n_out_gates = (RET_WIDTH + HG_WIDTH) // piece
    ret_items = [functools.partial(ret_item, p, ci) for ci in chunks for p in range(RET_HEADS // 2)]
    gate_items = [functools.partial(gate_item, i) for i in range(gate_ref.shape[1] // piece)]
    _interleave([[hgrn_item(h, ci) for h in range(HG_HEADS)] for ci in chunks],
                gate_items[:n_out_gates] + ret_items + gate_items[n_out_gates:])

    up_ret = _dot(act_ref[:, 0:RET_WIDTH], wru_ref[...])
    up_hg = _dot(act_ref[:, RET_WIDTH:], whu_ref[...])
    ga = gate_ref[:, 2 * RET_WIDTH:2 * RET_WIDTH + D_MODEL]
    gb_ = gate_ref[:, 2 * RET_WIDTH + D_MODEL:]
    merged = (ga * up_ret + gb_ * up_hg).astype(BF16)
    h1_ref[0] = x_ref[0] + _dot(merged, wout_ref[...])


def _gelu_tanh(x):
    return 0.5 * x * (1.0 + jnp.tanh(0.7978845608028654 * (x + 0.044715 * (x * x * x))))


def _ffn_kernel(hin_ref, hres_ref, nffn_ref, win_ref, cw_ref, cb_ref, wdown_ref, nfin_ref,
                y_ref, u_ref, v_ref, act_ref, *, tiles_per_seq):
    j = pl.program_id(0)
    tm = hin_ref.shape[1]
    slot = j % 2
    prev = 1 - slot

    @pl.when(j == 0)
    def _():
        u_ref[...] = jnp.zeros_like(u_ref)
        v_ref[...] = jnp.zeros_like(v_ref)
        act_ref[...] = jnp.zeros_like(act_ref)

    has_left = (j % tiles_per_seq) > 0
    has_right = ((j + tiles_per_seq - 1) % tiles_per_seq) < tiles_per_seq - 1
    hn = _rmsnorm(hin_ref[0], nffn_ref[...]).astype(BF16)

    def project(cb):
        cols = slice(cb * FFN_COLS, (cb + 1) * FFN_COLS)
        u_ref[slot, SUB:SUB + tm, cols] = _dot(hn, win_ref[:, cols])
        v_ref[slot, :, cols] = _dot(hn, win_ref[:, D_FF + cb * FFN_COLS:D_FF + (cb + 1) * FFN_COLS])
        u_ref[prev, SUB + tm:2 * SUB + tm, cols] = jnp.where(has_right, u_ref[slot, SUB:2 * SUB, cols],
                                                             0.0)
        u_ref[slot, 0:SUB, cols] = jnp.where(has_left, u_ref[prev, tm:SUB + tm, cols], 0.0)

    def activate(cb):
        cols = slice(cb * FFN_COLS, (cb + 1) * FFN_COLS)
        uc = (cb_ref[:, cols] + u_ref[prev, SUB - 1:SUB - 1 + tm, cols] * cw_ref[0:1, cols]
              + u_ref[prev, SUB:SUB + tm, cols] * cw_ref[1:2, cols]
              + u_ref[prev, SUB + 1:SUB + 1 + tm, cols] * cw_ref[2:3, cols])
        act_ref[prev, :, cols] = (_gelu_tanh(uc) * v_ref[prev, :, cols]).astype(BF16)

    h2 = []

    def finish():
        cols = slice(len(h2) * FFN_COLS, (len(h2) + 1) * FFN_COLS)
        h2.append(hres_ref[0, :, cols] + _dot(act_ref[slot], wdown_ref[:, cols]))

    n_in, n_out = D_FF // FFN_COLS, D_MODEL // FFN_COLS
    finish_before = {round(i * n_in / n_out) for i in range(n_out)}
    project(0)
    for cb in range(n_in):
        if cb in finish_before:
            finish()
        if cb + 1 < n_in:
            project(cb + 1)
        activate(cb)
    y_ref[0] = _rmsnorm(jnp.concatenate(h2, axis=1), nfin_ref[...])


def _const_spec(shape):
    zeros = (0,) * len(shape)
    return pl.BlockSpec(shape, lambda *_: zeros, pipeline_mode=pl.Buffered(1))


def _retention_tables(seq_len):
    half = RET_DK // 2
    inv_freq = ROPE_BASE ** (-jnp.arange(half, dtype=F32) / half)
    ang = jnp.arange(seq_len, dtype=F32)[:, None] * inv_freq[None, :]
    cos, sin = jnp.cos(ang), jnp.sin(ang)
    cos_t = jnp.tile(cos, (1, LANES // half))
    sin_t = jnp.tile(jnp.concatenate([-sin, sin], axis=1), (1, LANES // RET_DK))
    log_gamma = jnp.log(1.0 - 2.0 ** (-5.0 - jnp.arange(RET_HEADS, dtype=F32)))
    idx = jnp.arange(CHUNK, dtype=F32)
    dist = jnp.abs(idx[:, None] - idx[None, :])
    dmat = jnp.exp(dist[None] * log_gamma[:, None, None])
    lg_cols = jnp.repeat(log_gamma, RET_DK)[None, :]
    wq_f = jnp.exp((idx + 1.0)[:, None] * lg_cols)
    wk_f = jnp.exp((CHUNK - 1.0 - idx)[:, None] * lg_cols)
    wq_b = jnp.exp((CHUNK - idx)[:, None] * lg_cols)
    wk_b = jnp.exp(idx[:, None] * lg_cols)
    lane_head = jnp.arange(LANES) // RET_DK
    gdec = jnp.exp(CHUNK * log_gamma).reshape(RET_HEADS // 2, 2)
    same = lane_head[:, None] == lane_head[None, :]
    gblk = jnp.where(same[None], gdec[:, lane_head][:, :, None], 0.0)
    return cos_t, sin_t, dmat, wq_f, wk_f, wq_b, wk_b, gblk


def _encode(x, p):
    bsz, seq_len, _ = x.shape
    assert seq_len % TILE == 0 and seq_len % FFN_TILE == 0
    nt = seq_len // TILE
    cos_t, sin_t, dmat, wq_f, wk_f, wq_b, wk_b, gblk = _retention_tables(seq_len)
    tok = jnp.arange(CHUNK)
    tri_f = (tok[None, :] <= tok[:, None]).astype(BF16)
    params = pltpu.CompilerParams(dimension_semantics=("parallel", "arbitrary"),
                                  vmem_limit_bytes=VMEM_LIMIT)

    def tile_spec(width, rev=False):
        if rev:
            return pl.BlockSpec((1, TILE, width), lambda b, j: (b, nt - 1 - j, 0))
        return pl.BlockSpec((1, TILE, width), lambda b, j: (b, j, 0))

    def out(width, dtype):
        return jax.ShapeDtypeStruct((bsz, seq_len, width), dtype)

    state_scratch = [pltpu.VMEM((RET_HEADS // 2, LANES, LANES), F32),
                     pltpu.VMEM((HG_HEADS, LANES, LANES), F32)]

    n_gate = 2 * RET_WIDTH + 2 * D_MODEL
    pret, phg, hfb, o_fwd = pl.pallas_call(
        _mix_fwd_kernel,
        grid=(bsz, nt),
        in_specs=[tile_spec(D_MODEL), _const_spec((1, D_MODEL)), _const_spec((D_MODEL, D_IN)),
                  _const_spec((2, 2, HG_WIDTH)),
                  pl.BlockSpec((TILE, LANES), lambda b, j: (j, 0)),
                  pl.BlockSpec((TILE, LANES), lambda b, j: (j, 0)),
                  _const_spec((RET_HEADS, CHUNK, CHUNK)), _const_spec((CHUNK, RET_WIDTH)),
                  _const_spec((CHUNK, RET_WIDTH)), _const_spec((RET_HEADS // 2, LANES, LANES)),
                  _const_spec((CHUNK, CHUNK)), _const_spec((CHUNK, CHUNK))],
        out_specs=[tile_spec(3 * RET_WIDTH), tile_spec(2 * HG_WIDTH), tile_spec(HG_WIDTH),
                   tile_spec(RET_WIDTH + HG_WIDTH)],
        out_shape=[out(3 * RET_WIDTH, BF16), out(2 * HG_WIDTH, BF16), out(HG_WIDTH, F32),
                   out(RET_WIDTH + HG_WIDTH, F32)],
        scratch_shapes=state_scratch,
        compiler_params=params,
        name="mix_fwd",
    )(x, p["norm_mix"], p["w_in"], p["lower_bounds"], cos_t, sin_t, dmat, wq_f, wk_f, gblk,
      tri_f, _pair_codes(False))

    h1 = pl.pallas_call(
        _mix_bwd_kernel,
        grid=(bsz, nt),
        in_specs=[tile_spec(D_MODEL, True), tile_spec(3 * RET_WIDTH, True),
                  tile_spec(2 * HG_WIDTH, True), tile_spec(HG_WIDTH, True),
                  tile_spec(RET_WIDTH + HG_WIDTH, True),
                  _const_spec((1, D_MODEL)), _const_spec((D_MODEL, n_gate)),
                  _const_spec((2, 2, HG_WIDTH)), _const_spec((CHUNK, RET_WIDTH)),
                  _const_spec((CHUNK, RET_WIDTH)), _const_spec((RET_HEADS // 2, LANES, LANES)),
                  _const_spec((CHUNK, CHUNK)), _const_spec((CHUNK, CHUNK)),
                  _const_spec((1, RET_WIDTH)), _const_spec((1, HG_WIDTH)),
                  _const_spec((RET_WIDTH, D_MODEL)), _const_spec((HG_WIDTH, D_MODEL)),
                  _const_spec((D_MODEL, D_MODEL))],
        out_specs=tile_spec(D_MODEL, True),
        out_shape=out(D_MODEL, F32),
        scratch_shapes=state_scratch + [pltpu.VMEM((TILE, RET_WIDTH + HG_WIDTH), BF16),
                                        pltpu.VMEM((TILE, n_gate), F32)],
        compiler_params=params,
        name="mix_bwd",
    )(x, pret, phg, hfb, o_fwd, p["norm_mix"], p["w_gates"], p["lower_bounds"], wq_b, wk_b, gblk,
      tri_f.T, _pair_codes(True),
      p["w_ret_norm"], p["w_hg_norm"], p["w_ret_up"], p["w_hg_up"], p["w_out"])

    nf = seq_len // FFN_TILE
    n_tiles = bsz * nf

    def lagged_tile(lag):
        def index_map(j):
            tile = jnp.clip(j - lag, 0, n_tiles - 1)
            return (tile // nf, tile % nf, 0)
        return pl.BlockSpec((1, FFN_TILE, D_MODEL), index_map)

    return pl.pallas_call(
        functools.partial(_ffn_kernel, tiles_per_seq=nf),
        grid=(n_tiles + 2,),
        in_specs=[lagged_tile(0), lagged_tile(2),
                  _const_spec((1, D_MODEL)), _const_spec((D_MODEL, 2 * D_FF)),
                  _const_spec((3, D_FF)), _const_spec((1, D_FF)), _const_spec((D_FF, D_MODEL)),
                  _const_spec((1, D_MODEL))],
        out_specs=lagged_tile(2),
        out_shape=out(D_MODEL, F32),
        scratch_shapes=[pltpu.VMEM((2, FFN_TILE + 2 * SUB, D_FF), F32),
                        pltpu.VMEM((2, FFN_TILE, D_FF), F32),
                        pltpu.VMEM((2, FFN_TILE, D_FF), BF16)],
        compiler_params=pltpu.CompilerParams(dimension_semantics=("arbitrary",),
                                             vmem_limit_bytes=VMEM_LIMIT),
        name="conv_ffn",
    )(h1, h1, p["norm_ffn"], p["w_ffn_in"], p["conv_w"], p["conv_b"], p["w_ffn_down"],
      p["norm_final"])


def kernel(x_prompt, x_sample, w_in, lower_bounds, w_ret_norm, w_hg_norm, w_ret_up, w_hg_up, w_out,
           norm_mix, norm_ffn, w_ffn_in, conv_w, conv_b, w_ffn_down, norm_final):
    assert w_in.shape[0] == 1, "one layer"
    p = {
        "w_in": w_in[0].astype(BF16),
        "w_gates": jnp.concatenate([w_in[0, :, C_RG:C_RG + RET_WIDTH], w_in[0, :, C_HG:]],
                                   axis=1).astype(BF16),
        "lower_bounds": lower_bounds.astype(F32),
        "w_ret_norm": w_ret_norm.astype(F32),
        "w_hg_norm": w_hg_norm.astype(F32),
        "w_ret_up": w_ret_up[0].astype(BF16),
        "w_hg_up": w_hg_up[0].astype(BF16),
        "w_out": w_out[0].astype(BF16),
        "norm_mix": norm_mix.astype(F32),
        "norm_ffn": norm_ffn.astype(F32),
        "w_ffn_in": w_ffn_in[0].astype(BF16),
        "conv_w": conv_w[0].astype(F32),
        "conv_b": conv_b.astype(F32),
        "w_ffn_down": w_ffn_down[0].astype(BF16),
        "norm_final": norm_final.reshape(1, D_MODEL).astype(F32),
    }
    return _encode(x_prompt, p), _encode(x_sample, p)
```

```python
import functools

import jax
import jax.numpy as jnp
from jax import lax
from jax.experimental import pallas as pl
from jax.experimental.pallas import tpu as pltpu

F32 = jnp.float32
BF16 = jnp.bfloat16

D_MODEL = 1024
RET_HEADS = 8
RET_DK = 64
RET_WIDTH = 512
HG_HEADS = 4
HG_DK = 128
HG_WIDTH = 512
D_FF = 2816
EPS = 1e-6
ROPE_BASE = 10000.0

C_RQ, C_RK, C_RV, C_RG = 0, 512, 1024, 1536
C_HQ, C_HFF, C_HFB, C_HI, C_HG = 2048, 2560, 3072, 3584, 4096
C_GA, C_GB, D_IN = 4608, 5632, 6656

LANES = 128
SUB = 8
CHUNK = 128
TILE = 512
FFN_TILE = 256
FFN_COLS = 256
VMEM_LIMIT = 56 * 1024 * 1024


def _dot(a, b):
    return jnp.dot(a, b, preferred_element_type=F32)


def _dot_nt(a, b):
    return lax.dot_general(a, b, (((1,), (1,)), ((), ())), preferred_element_type=F32)


def _dot_tn(a, b):
    return lax.dot_general(a, b, (((0,), (0,)), ((), ())), preferred_element_type=F32)


def _sigmoid(x):
    return 1.0 / (1.0 + jnp.exp(-x))


def _silu(x):
    return x * _sigmoid(x)


def _rmsnorm(x, w):
    return x * lax.rsqrt(jnp.mean(x * x, axis=-1, keepdims=True) + EPS) * w


def _iota(shape, dim):
    return lax.broadcasted_iota(jnp.int32, shape, dim)


def _lower_bound(lb_ref, direction):
    a0 = lb_ref[0, direction:direction + 1, :]
    a1 = lb_ref[1, direction:direction + 1, :]
    m = jnp.maximum(a0, a1)
    e0 = jnp.exp(a0 - m)
    e1 = jnp.exp(a1 - m)
    return e0 / (e0 + e1)


def _forget(pre, lower):
    f = lower + (1.0 - lower) * _sigmoid(pre)
    return 1.0 - f, jnp.log2(f)


def _rotary(x, cos, sin_signed):
    first_half = (_iota(x.shape, 1) % RET_DK) < (RET_DK // 2)
    swapped = jnp.where(first_half, pltpu.roll(x, LANES - RET_DK // 2, axis=1),
                        pltpu.roll(x, RET_DK // 2, axis=1))
    return x * cos + swapped * sin_signed


def _pair_codes(rev):
    r = jnp.arange(CHUNK, dtype=jnp.int32)[:, None]
    c = jnp.arange(CHUNK, dtype=jnp.int32)[None, :]
    if rev:
        r, c = c, r
    code = jnp.where((r // SUB == c // SUB) & (r >= c), r - c, -1)
    s, level = SUB, 0
    while s < CHUNK:
        sibling = (r // (2 * s) == c // (2 * s)) & (r // s == c // s + 1)
        code = jnp.where(sibling, SUB + level, code)
        s, level = 2 * s, level + 1
    return code


def _hgrn_chunk(load, emit, st_ref, head, tri_ref, code_ref, rev, fillers=(1, 1, 1)):
    q, k, g, v = load()
    c = q.shape[0]
    nsub = c // SUB
    g_hi = g.astype(BF16)
    g_lo = (g - g_hi.astype(F32)).astype(BF16)
    b = _dot(tri_ref[...], g_hi) + _dot(tri_ref[...], g_lo)
    yield fillers[0]

    amat = jnp.where(code_ref[...] == 0, jnp.sum(q * k, axis=-1, keepdims=True), 0.0)
    q3 = q.reshape(nsub, SUB, LANES)
    k3 = k.reshape(nsub, SUB, LANES)
    b3 = b.reshape(nsub, SUB, LANES)
    for d in range(1, SUB):
        shift = (SUB - d) if rev else d
        ks = pltpu.roll(k3, shift, axis=1)
        bs = pltpu.roll(b3, shift, axis=1)
        a = jnp.sum(q3 * ks * jnp.exp2(b3 - bs), axis=-1, keepdims=True)
        amat = jnp.where(code_ref[...] == d, a.reshape(c, 1), amat)
    yield fillers[1]

    level_dots = []
    s = SUB
    while s < c:
        nblk = c // s
        if rev:
            edge = [jnp.broadcast_to(b[m * s:m * s + 1, :], (s, LANES)) for m in range(nblk)]
            q_edge = jnp.concatenate(edge[1:] + [jnp.zeros((s, LANES), F32)], axis=0)
        else:
            edge = [jnp.broadcast_to(b[m * s + s - 1:m * s + s, :], (s, LANES)) for m in range(nblk)]
            q_edge = jnp.concatenate([jnp.zeros((s, LANES), F32)] + edge[:-1], axis=0)
        k_edge = jnp.concatenate(edge, axis=0)
        qt = (q * jnp.exp2(b - q_edge)).astype(BF16)
        kt = (k * jnp.exp2(k_edge - b)).astype(BF16)
        level_dots.append(_dot_nt(qt, kt))
        s *= 2

    vb = v.astype(BF16)
    b_tot = b[0:1, :] if rev else b[c - 1:c, :]
    st = st_ref[head]
    o_cross = _dot_nt((q * jnp.exp2(b)).astype(BF16), st.astype(BF16))
    ds_t = _dot_tn(vb, (k * jnp.exp2(b_tot - b)).astype(BF16))
    st_ref[head] = st * jnp.exp2(b_tot) + ds_t
    yield fillers[2]

    for level, a in enumerate(level_dots):
        amat = jnp.where(code_ref[...] == SUB + level, a, amat)
    emit(o_cross + _dot(amat.astype(BF16), vb))


def _interleave(groups, fillers):
    fillers = list(fillers)
    for group in groups:
        active = list(group)
        while active:
            still, want = [], 1
            for gen in active:
                try:
                    want = max(want, next(gen))
                    still.append(gen)
                except StopIteration:
                    pass
            active = still
            for filler in fillers[:want]:
                filler()
            fillers = fillers[want:]
    for filler in fillers:
        filler()


def _ret_state_step(q, k, vb, wq, wk, gblk, st_ref, pair):
    st = st_ref[pair]
    o = _dot((q * wq).astype(BF16), st.astype(BF16))
    ds = _dot_tn((k * wk).astype(BF16), vb)
    same_head = (_iota((LANES, LANES), 0) < RET_DK) == (_iota((LANES, LANES), 1) < RET_DK)
    st_ref[pair] = st * gblk + jnp.where(same_head, ds, 0.0)
    return o


def _mix_fwd_kernel(x_ref, nmix_ref, w_ref, lb_ref, cos_ref, sin_ref, dmat_ref, wq_ref, wk_ref,
                    gblk_ref, tri_ref, code_ref, pret_ref, phg_ref, hfb_ref, of_ref,
                    sret_ref, shg_ref):
    @pl.when(pl.program_id(1) == 0)
    def _():
        sret_ref[...] = jnp.zeros_like(sret_ref)
        shg_ref[...] = jnp.zeros_like(shg_ref)

    tm = x_ref.shape[1]
    nchunk = tm // CHUNK
    hn = _rmsnorm(x_ref[0], nmix_ref[...]).astype(BF16)

    def proj(lo, width):
        half = tm // 2
        return jnp.concatenate([_dot(hn[0:half], w_ref[:, lo:lo + width]),
                                _dot(hn[half:], w_ref[:, lo:lo + width])], axis=0)

    q2 = _silu(proj(C_HQ, HG_WIDTH))
    vi = proj(C_HI, HG_WIDTH)
    phg_ref[0, :, 0:HG_WIDTH] = q2.astype(BF16)
    phg_ref[0, :, HG_WIDTH:] = vi.astype(BF16)
    kf, gf = _forget(proj(C_HFF, HG_WIDTH), _lower_bound(lb_ref, 0))

    def hgrn_item(h, ci):
        cols = slice(h * LANES, (h + 1) * LANES)
        rows = slice(ci * CHUNK, (ci + 1) * CHUNK)

        def load():
            return q2[rows, cols], kf[rows, cols], gf[rows, cols], vi[rows, cols]

        def emit(o):
            of_ref[0, rows, RET_WIDTH + h * LANES:RET_WIDTH + (h + 1) * LANES] = o

        return _hgrn_chunk(load, emit, shg_ref, h, tri_ref, code_ref, rev=False, fillers=(2, 2, 1))

    piece = 2 * LANES

    def hfb_item(i):
        hfb_ref[0, :, i * piece:(i + 1) * piece] = proj(C_HFB + i * piece, piece)

    ret_qkv = {}

    def ret_proj_item(group):
        pq = proj(C_RQ + group * piece, piece)
        pk = proj(C_RK + group * piece, piece)
        pv = proj(C_RV + group * piece, piece)
        cos = cos_ref[...]
        sin = sin_ref[...]
        for pi in range(2):
            p = 2 * group + pi
            lcols = slice(pi * LANES, (pi + 1) * LANES)
            q = _rotary(pq[:, lcols], cos, sin)
            k = _rotary(pk[:, lcols], cos, sin) * (RET_DK ** -0.5)
            v = pv[:, lcols]
            pret_ref[0, :, p * LANES:(p + 1) * LANES] = q.astype(BF16)
            pret_ref[0, :, RET_WIDTH + p * LANES:RET_WIDTH + (p + 1) * LANES] = k.astype(BF16)
            pret_ref[0, :, 2 * RET_WIDTH + p * LANES:2 * RET_WIDTH + (p + 1) * LANES] = v.astype(BF16)
            ret_qkv[p] = (q, k, v)

    def ret_pair_item(p, ci):
        q, k, v = ret_qkv[p]
        cols = slice(p * LANES, (p + 1) * LANES)
        head0 = _iota((CHUNK, LANES), 1) < RET_DK
        rows = slice(ci * CHUNK, (ci + 1) * CHUNK)
        qc, kc, vc = q[rows], k[rows], v[rows]
        kb = kc.astype(BF16)
        vb = vc.astype(BF16)
        s0 = _dot_nt(jnp.where(head0, qc, 0.0).astype(BF16), kb) * dmat_ref[2 * p]
        s1 = _dot_nt(jnp.where(head0, 0.0, qc).astype(BF16), kb) * dmat_ref[2 * p + 1]
        pcat = jnp.concatenate([s0.astype(BF16), s1.astype(BF16)], axis=1)
        zero = jnp.zeros_like(vb)
        vblk = jnp.concatenate([jnp.where(head0, vb, zero), jnp.where(head0, zero, vb)], axis=0)
        o = _dot(pcat, vblk)
        o = o + _ret_state_step(qc, kc, vb, wq_ref[:, cols], wk_ref[:, cols], gblk_ref[p],
                                sret_ref, p)
        of_ref[0, rows, cols] = o

    mxu_items = []
    for group in range(RET_HEADS // 4):
        mxu_items.append(functools.partial(ret_proj_item, group))
        mxu_items += [functools.partial(ret_pair_item, p, ci)
                      for p in (2 * group, 2 * group + 1) for ci in range(nchunk)]
        mxu_items.append(functools.partial(hfb_item, group))
    _interleave([[hgrn_item(h, ci) for h in range(HG_HEADS)] for ci in range(nchunk)], mxu_items)


def _mix_bwd_kernel(x_ref, pret_ref, phg_ref, hfb_ref, of_ref, nmix_ref, wg_ref, lb_ref, wq_ref,
                    wk_ref, gblk_ref, tri_ref, code_ref, rnorm_ref, hnorm_ref, wru_ref, whu_ref,
                    wout_ref, h1_ref, sret_ref, shg_ref, act_ref, gate_ref):
    @pl.when(pl.program_id(1) == 0)
    def _():
        sret_ref[...] = jnp.zeros_like(sret_ref)
        shg_ref[...] = jnp.zeros_like(shg_ref)

    tm = x_ref.shape[1]
    nchunk = tm // CHUNK
    head0 = _iota((CHUNK, LANES), 1) < RET_DK
    hn = _rmsnorm(x_ref[0], nmix_ref[...]).astype(BF16)
    piece = 2 * LANES

    def gate_item(i):
        cols = slice(i * piece, (i + 1) * piece)
        pre = _dot(hn, wg_ref[:, cols])
        gate_ref[:, cols] = _silu(pre) if (i + 1) * piece <= RET_WIDTH + HG_WIDTH else _sigmoid(pre)

    def ret_item(p, ci):
        cols = slice(p * LANES, (p + 1) * LANES)
        rows = slice(ci * CHUNK, (ci + 1) * CHUNK)
        qc = pret_ref[0, rows, p * LANES:(p + 1) * LANES].astype(F32)
        kc = pret_ref[0, rows, RET_WIDTH + p * LANES:RET_WIDTH + (p + 1) * LANES].astype(F32)
        vb = pret_ref[0, rows, 2 * RET_WIDTH + p * LANES:2 * RET_WIDTH + (p + 1) * LANES]
        o = of_ref[0, rows, cols] + _ret_state_step(qc, kc, vb, wq_ref[:, cols], wk_ref[:, cols],
                                                    gblk_ref[p], sret_ref, p)
        sq = o * o
        ms0 = jnp.sum(jnp.where(head0, sq, 0.0), axis=-1, keepdims=True) * (1.0 / RET_DK)
        ms1 = jnp.sum(jnp.where(head0, 0.0, sq), axis=-1, keepdims=True) * (1.0 / RET_DK)
        o = o * jnp.where(head0, lax.rsqrt(ms0 + EPS), lax.rsqrt(ms1 + EPS))
        act_ref[rows, cols] = (o * rnorm_ref[:, cols] * gate_ref[rows, cols]).astype(BF16)

    lower = _lower_bound(lb_ref, 1)

    def hgrn_item(h, ci):
        cols = slice(h * LANES, (h + 1) * LANES)
        hcols = slice(RET_WIDTH + h * LANES, RET_WIDTH + (h + 1) * LANES)
        rows = slice(ci * CHUNK, (ci + 1) * CHUNK)

        def load():
            kb, gb = _forget(hfb_ref[0, rows, cols], lower[:, cols])
            q2 = phg_ref[0, rows, cols].astype(F32)
            vi = phg_ref[0, rows, HG_WIDTH + h * LANES:HG_WIDTH + (h + 1) * LANES].astype(F32)
            return q2, kb, gb, vi

        def emit(o_bwd):
            o = of_ref[0, rows, hcols] + o_bwd
            o = o * lax.rsqrt(jnp.mean(o * o, axis=-1, keepdims=True) + EPS)
            act_ref[rows, hcols] = (o * hnorm_ref[:, cols] * gate_ref[rows, hcols]).astype(BF16)

        return _hgrn_chunk(load, emit, shg_ref, h, tri_ref, code_ref, rev=True, fillers=(2, 3, 2))

    chunks = list(reversed(range(nchunk)))
    n_out_gates = (RET_WIDTH + HG_WIDTH) // piece
    ret_items = [functools.partial(ret_item, p, ci) for ci in chunks for p in range(RET_HEADS // 2)]
    gate_items = [functools.partial(gate_item, i) for i in range(gate_ref.shape[1] // piece)]
    _interleave([[hgrn_item(h, ci) for h in range(HG_HEADS)] for ci in chunks],
                gate_items[:n_out_gates] + ret_items + gate_items[n_out_gates:])

    up_ret = _dot(act_ref[:, 0:RET_WIDTH], wru_ref[...])
    up_hg = _dot(act_ref[:, RET_WIDTH:], whu_ref[...])
    ga = gate_ref[:, 2 * RET_WIDTH:2 * RET_WIDTH + D_MODEL]
    gb_ = gate_ref[:, 2 * RET_WIDTH + D_MODEL:]
    merged = (ga * up_ret + gb_ * up_hg).astype(BF16)
    h1_ref[0] = x_ref[0] + _dot(merged, wout_ref[...])


def _gelu_tanh(x):
    return 0.5 * x * (1.0 + jnp.tanh(0.7978845608028654 * (x + 0.044715 * (x * x * x))))


def _ffn_kernel(hin_ref, hres_ref, nffn_ref, win_ref, cw_ref, cb_ref, wdown_ref, nfin_ref,
                y_ref, u_ref, v_ref, act_ref, *, tiles_per_seq):
    j = pl.program_id(0)
    tm = hin_ref.shape[1]
    slot = j % 2
    prev = 1 - slot

    @pl.when(j == 0)
    def _():
        u_ref[...] = jnp.zeros_like(u_ref)
        v_ref[...] = jnp.zeros_like(v_ref)
        act_ref[...] = jnp.zeros_like(act_ref)

    has_left = (j % tiles_per_seq) > 0
    has_right = ((j + tiles_per_seq - 1) % tiles_per_seq) < tiles_per_seq - 1
    hn = _rmsnorm(hin_ref[0], nffn_ref[...]).astype(BF16)

    def project(cb):
        cols = slice(cb * FFN_COLS, (cb + 1) * FFN_COLS)
        u_ref[slot, SUB:SUB + tm, cols] = _dot(hn, win_ref[:, cols])
        v_ref[slot, :, cols] = _dot(hn, win_ref[:, D_FF + cb * FFN_COLS:D_FF + (cb + 1) * FFN_COLS])
        u_ref[prev, SUB + tm:2 * SUB + tm, cols] = jnp.where(has_right, u_ref[slot, SUB:2 * SUB, cols],
                                                             0.0)
        u_ref[slot, 0:SUB, cols] = jnp.where(has_left, u_ref[prev, tm:SUB + tm, cols], 0.0)

    def activate(cb):
        cols = slice(cb * FFN_COLS, (cb + 1) * FFN_COLS)
        uc = (cb_ref[:, cols] + u_ref[prev, SUB - 1:SUB - 1 + tm, cols] * cw_ref[0:1, cols]
              + u_ref[prev, SUB:SUB + tm, cols] * cw_ref[1:2, cols]
              + u_ref[prev, SUB + 1:SUB + 1 + tm, cols] * cw_ref[2:3, cols])
        act_ref[prev, :, cols] = (_gelu_tanh(uc) * v_ref[prev, :, cols]).astype(BF16)

    h2 = []

    def finish():
        cols = slice(len(h2) * FFN_COLS, (len(h2) + 1) * FFN_COLS)
        h2.append(hres_ref[0, :, cols] + _dot(act_ref[slot], wdown_ref[:, cols]))

    n_in, n_out = D_FF // FFN_COLS, D_MODEL // FFN_COLS
    finish_before = {round(i * n_in / n_out) for i in range(n_out)}
    project(0)
    for cb in range(n_in):
        if cb in finish_before:
            finish()
        if cb + 1 < n_in:
            project(cb + 1)
        activate(cb)
    y_ref[0] = _rmsnorm(jnp.concatenate(h2, axis=1), nfin_ref[...])


def _const_spec(shape):
    zeros = (0,) * len(shape)
    return pl.BlockSpec(shape, lambda *_: zeros, pipeline_mode=pl.Buffered(1))


def _retention_tables(seq_len):
    half = RET_DK // 2
    inv_freq = ROPE_BASE ** (-jnp.arange(half, dtype=F32) / half)
    ang = jnp.arange(seq_len, dtype=F32)[:, None] * inv_freq[None, :]
    cos, sin = jnp.cos(ang), jnp.sin(ang)
    cos_t = jnp.tile(cos, (1, LANES // half))
    sin_t = jnp.tile(jnp.concatenate([-sin, sin], axis=1), (1, LANES // RET_DK))
    log_gamma = jnp.log(1.0 - 2.0 ** (-5.0 - jnp.arange(RET_HEADS, dtype=F32)))
    idx = jnp.arange(CHUNK, dtype=F32)
    dist = jnp.abs(idx[:, None] - idx[None, :])
    dmat = jnp.exp(dist[None] * log_gamma[:, None, None])
    lg_cols = jnp.repeat(log_gamma, RET_DK)[None, :]
    wq_f = jnp.exp((idx + 1.0)[:, None] * lg_cols)
    wk_f = jnp.exp((CHUNK - 1.0 - idx)[:, None] * lg_cols)
    wq_b = jnp.exp((CHUNK - idx)[:, None] * lg_cols)
    wk_b = jnp.exp(idx[:, None] * lg_cols)
    lane_head = jnp.arange(LANES) // RET_DK
    gdec = jnp.exp(CHUNK * log_gamma).reshape(RET_HEADS // 2, 2)
    same = lane_head[:, None] == lane_head[None, :]
    gblk = jnp.where(same[None], gdec[:, lane_head][:, :, None], 0.0)
    return cos_t, sin_t, dmat, wq_f, wk_f, wq_b, wk_b, gblk


def _encode(x, p):
    bsz, seq_len, _ = x.shape
    assert seq_len % TILE == 0 and seq_len % FFN_TILE == 0
    nt = seq_len // TILE
    cos_t, sin_t, dmat, wq_f, wk_f, wq_b, wk_b, gblk = _retention_tables(seq_len)
    tok = jnp.arange(CHUNK)
    tri_f = (tok[None, :] <= tok[:, None]).astype(BF16)
    params = pltpu.CompilerParams(dimension_semantics=("parallel", "arbitrary"),
                                  vmem_limit_bytes=VMEM_LIMIT)

    def tile_spec(width, rev=False):
        if rev:
            return pl.BlockSpec((1, TILE, width), lambda b, j: (b, nt - 1 - j, 0))
        return pl.BlockSpec((1, TILE, width), lambda b, j: (b, j, 0))

    def out(width, dtype):
        return jax.ShapeDtypeStruct((bsz, seq_len, width), dtype)

    state_scratch = [pltpu.VMEM((RET_HEADS // 2, LANES, LANES), F32),
                     pltpu.VMEM((HG_HEADS, LANES, LANES), F32)]

    n_gate = 2 * RET_WIDTH + 2 * D_MODEL
    pret, phg, hfb, o_fwd = pl.pallas_call(
        _mix_fwd_kernel,
        grid=(bsz, nt),
        in_specs=[tile_spec(D_MODEL), _const_spec((1, D_MODEL)), _const_spec((D_MODEL, D_IN)),
                  _const_spec((2, 2, HG_WIDTH)),
                  pl.BlockSpec((TILE, LANES), lambda b, j: (j, 0)),
                  pl.BlockSpec((TILE, LANES), lambda b, j: (j, 0)),
                  _const_spec((RET_HEADS, CHUNK, CHUNK)), _const_spec((CHUNK, RET_WIDTH)),
                  _const_spec((CHUNK, RET_WIDTH)), _const_spec((RET_HEADS // 2, LANES, LANES)),
                  _const_spec((CHUNK, CHUNK)), _const_spec((CHUNK, CHUNK))],
        out_specs=[tile_spec(3 * RET_WIDTH), tile_spec(2 * HG_WIDTH), tile_spec(HG_WIDTH),
                   tile_spec(RET_WIDTH + HG_WIDTH)],
        out_shape=[out(3 * RET_WIDTH, BF16), out(2 * HG_WIDTH, BF16), out(HG_WIDTH, F32),
                   out(RET_WIDTH + HG_WIDTH, F32)],
        scratch_shapes=state_scratch,
        compiler_params=params,
        name="mix_fwd",
    )(x, p["norm_mix"], p["w_in"], p["lower_bounds"], cos_t, sin_t, dmat, wq_f, wk_f, gblk,
      tri_f, _pair_codes(False))

    h1 = pl.pallas_call(
        _mix_bwd_kernel,
        grid=(bsz, nt),
        in_specs=[tile_spec(D_MODEL, True), tile_spec(3 * RET_WIDTH, True),
                  tile_spec(2 * HG_WIDTH, True), tile_spec(HG_WIDTH, True),
                  tile_spec(RET_WIDTH + HG_WIDTH, True),
                  _const_spec((1, D_MODEL)), _const_spec((D_MODEL, n_gate)),
                  _const_spec((2, 2, HG_WIDTH)), _const_spec((CHUNK, RET_WIDTH)),
                  _const_spec((CHUNK, RET_WIDTH)), _const_spec((RET_HEADS // 2, LANES, LANES)),
                  _const_spec((CHUNK, CHUNK)), _const_spec((CHUNK, CHUNK)),
                  _const_spec((1, RET_WIDTH)), _const_spec((1, HG_WIDTH)),
                  _const_spec((RET_WIDTH, D_MODEL)), _const_spec((HG_WIDTH, D_MODEL)),
                  _const_spec((D_MODEL, D_MODEL))],
        out_specs=tile_spec(D_MODEL, True),
        out_shape=out(D_MODEL, F32),
        scratch_shapes=state_scratch + [pltpu.VMEM((TILE, RET_WIDTH + HG_WIDTH), BF16),
                                        pltpu.VMEM((TILE, n_gate), F32)],
        compiler_params=params,
        name="mix_bwd",
    )(x, pret, phg, hfb, o_fwd, p["norm_mix"], p["w_gates"], p["lower_bounds"], wq_b, wk_b, gblk,
      tri_f.T, _pair_codes(True),
      p["w_ret_norm"], p["w_hg_norm"], p["w_ret_up"], p["w_hg_up"], p["w_out"])

    nf = seq_len // FFN_TILE
    n_tiles = bsz * nf

    def lagged_tile(lag):
        def index_map(j):
            tile = jnp.clip(j - lag, 0, n_tiles - 1)
            return (tile // nf, tile % nf, 0)
        return pl.BlockSpec((1, FFN_TILE, D_MODEL), index_map)

    return pl.pallas_call(
        functools.partial(_ffn_kernel, tiles_per_seq=nf),
        grid=(n_tiles + 2,),
        in_specs=[lagged_tile(0), lagged_tile(2),
                  _const_spec((1, D_MODEL)), _const_spec((D_MODEL, 2 * D_FF)),
                  _const_spec((3, D_FF)), _const_spec((1, D_FF)), _const_spec((D_FF, D_MODEL)),
                  _const_spec((1, D_MODEL))],
        out_specs=lagged_tile(2),
        out_shape=out(D_MODEL, F32),
        scratch_shapes=[pltpu.VMEM((2, FFN_TILE + 2 * SUB, D_FF), F32),
                        pltpu.VMEM((2, FFN_TILE, D_FF), F32),
                        pltpu.VMEM((2, FFN_TILE, D_FF), BF16)],
        compiler_params=pltpu.CompilerParams(dimension_semantics=("arbitrary",),
                                             vmem_limit_bytes=VMEM_LIMIT),
        name="conv_ffn",
    )(h1, h1, p["norm_ffn"], p["w_ffn_in"], p["conv_w"], p["conv_b"], p["w_ffn_down"],
      p["norm_final"])


def kernel(x_prompt, x_sample, w_in, lower_bounds, w_ret_norm, w_hg_norm, w_ret_up, w_hg_up, w_out,
           norm_mix, norm_ffn, w_ffn_in, conv_w, conv_b, w_ffn_down, norm_final):
    assert w_in.shape[0] == 1, "one layer"
    p = {
        "w_in": w_in[0].astype(BF16),
        "w_gates": jnp.concatenate([w_in[0, :, C_RG:C_RG + RET_WIDTH], w_in[0, :, C_HG:]],
                                   axis=1).astype(BF16),
        "lower_bounds": lower_bounds.astype(F32),
        "w_ret_norm": w_ret_norm.astype(F32),
        "w_hg_norm": w_hg_norm.astype(F32),
        "w_ret_up": w_ret_up[0].astype(BF16),
        "w_hg_up": w_hg_up[0].astype(BF16),
        "w_out": w_out[0].astype(BF16),
        "norm_mix": norm_mix.astype(F32),
        "norm_ffn": norm_ffn.astype(F32),
        "w_ffn_in": w_ffn_in[0].astype(BF16),
        "conv_w": conv_w[0].astype(F32),
        "conv_b": conv_b.astype(F32),
        "w_ffn_down": w_ffn_down[0].astype(BF16),
        "norm_final": norm_final.reshape(1, D_MODEL).astype(F32),
    }
    return _encode(x_prompt, p), _encode(x_sample, p)
```
